```python
import math
import jax, jax.numpy as jnp
from jax import lax
import numpy as np

D_MODEL = 4096
BATCH = 4
SEQ = 4096
DEPTH = 4

N_A_LAYERS = DEPTH // 2
N_B_LAYERS = DEPTH - N_A_LAYERS
N_DENSE = (DEPTH + 1) // 2
N_MOE = DEPTH // 2
RET_HEADS = 16
RET_QK_DIM = D_MODEL // RET_HEADS
RET_V_DIM = 2 * D_MODEL // RET_HEADS
RET_CHUNK = 128
DIFF_HEADS = 16
DIFF_HEAD_DIM = D_MODEL // (2 * DIFF_HEADS)
DIFF_V_DIM = 2 * DIFF_HEAD_DIM
Q_BLOCK = 128
ROPE_THETA = 10000.0
D_FF = 2 * D_MODEL
N_EXPERTS = 8
TOP_K = 2
D_EXPERT = D_MODEL // 4
PLE_DIM = 256
LN_EPS = 1e-5
DEEPNORM_ALPHA = (2.0 * DEPTH) ** 0.25
DEEPNORM_BETA = (8.0 * DEPTH) ** -0.25

kernel_name = "yoco_retention_diffattn_moe_deepnorm"


def layer_norm(x, g, b):
    xf = x.astype(jnp.float32)
    mu = jnp.mean(xf, axis=-1, keepdims=True)
    var = jnp.mean(jnp.square(xf - mu), axis=-1, keepdims=True)
    y = (xf - mu) * lax.rsqrt(var + LN_EPS) * g.astype(jnp.float32) + b.astype(jnp.float32)
    return y.astype(x.dtype)


def head_norm(x):
    xf = x.astype(jnp.float32)
    mu = jnp.mean(xf, axis=-1, keepdims=True)
    var = jnp.mean(jnp.square(xf - mu), axis=-1, keepdims=True)
    return ((xf - mu) * lax.rsqrt(var + LN_EPS)).astype(x.dtype)


def rms_norm(x, g):
    xf = x.astype(jnp.float32)
    y = xf * lax.rsqrt(jnp.mean(jnp.square(xf), axis=-1, keepdims=True) + LN_EPS)
    return (y * g.astype(jnp.float32)).astype(x.dtype)


def rope_tables(seq, dim):
    inv_freq = 1.0 / (ROPE_THETA ** (jnp.arange(0, dim, 2, dtype=jnp.float32) / dim))
    ang = jnp.arange(seq, dtype=jnp.float32)[:, None] * inv_freq[None, :]
    return jnp.cos(ang), jnp.sin(ang)


def apply_rope(x, cos, sin):
    x1, x2 = jnp.split(x, 2, axis=-1)
    c = cos[None, :, None, :].astype(x.dtype)
    s = sin[None, :, None, :].astype(x.dtype)
    return jnp.concatenate([x1 * c - x2 * s, x1 * s + x2 * c], axis=-1)


def retention(x, w_in, w_out, cos, sin):
    B, S, _ = x.shape
    H, dk, dv, C = RET_HEADS, RET_QK_DIM, RET_V_DIM, RET_CHUNK
    n_chunks = S // C
    proj = x @ w_in
    q, k, v, g = jnp.split(proj, [H * dk, 2 * H * dk, 2 * H * dk + H * dv], axis=-1)
    q = apply_rope(q.reshape(B, S, H, dk), cos, sin)
    k = apply_rope(k.reshape(B, S, H, dk), cos, sin) * (dk ** -0.5)
    v = v.reshape(B, S, H, dv)
    log_gamma = jnp.log1p(-jnp.exp2(-5.0 - jnp.arange(H, dtype=jnp.float32)))
    pos = jnp.arange(C, dtype=jnp.float32)
    rel = pos[:, None] - pos[None, :]
    decay_mask = jnp.where(rel >= 0, jnp.exp(log_gamma[:, None, None] * jnp.maximum(rel, 0.0)), 0.0)
    q_decay = jnp.exp(log_gamma[:, None] * (pos + 1.0))[:, :, None]
    k_decay = jnp.exp(log_gamma[:, None] * (C - 1.0 - pos))[:, :, None]
    chunk_decay = jnp.exp(log_gamma * C)[:, None, None]
    dt = x.dtype
    decay_mask, q_decay, k_decay, chunk_decay = (t.astype(dt) for t in (decay_mask, q_decay, k_decay, chunk_decay))

    def to_chunks(t):
        return t.reshape(B, n_chunks, C, H, t.shape[-1]).transpose(1, 0, 3, 2, 4)

    qc, kc, vc = to_chunks(q), to_chunks(k), to_chunks(v)

    def step(state, inp):
        q_i, k_i, v_i = inp
        scores = jnp.einsum('bhcd,bhmd->bhcm', q_i, k_i) * decay_mask
        inner = jnp.einsum('bhcm,bhme->bhce', scores, v_i)
        cross = jnp.einsum('bhcd,bhde->bhce', q_i * q_decay, state)
        new_state = state * chunk_decay + jnp.einsum('bhmd,bhme->bhde', k_i * k_decay, v_i)
        return new_state, inner + cross

    state0 = jnp.zeros((B, H, dk, dv), dt)
    _, out = lax.scan(step, state0, (qc, kc, vc))
    out = out.transpose(1, 0, 3, 2, 4).reshape(B, S, H, dv)
    out = head_norm(out).reshape(B, S, H * dv)
    return (jax.nn.silu(g) * out) @ w_out


def shared_kv(x_kv, w_kv, cos, sin):
    B, S, _ = x_kv.shape
    H, d, dv = DIFF_HEADS, DIFF_HEAD_DIM, DIFF_V_DIM
    kv = x_kv @ w_kv
    k, v = jnp.split(kv, [2 * H * d], axis=-1)
    k = apply_rope(k.reshape(B, S, 2 * H, d), cos, sin).reshape(B, S, H, 2, d)
    v = v.reshape(B, S, H, dv)
    return k, v


def diff_attention(x, w_q, w_out, lam, subln_g, k_sh, v_sh, cos, sin, lambda_init):
    B, S, _ = x.shape
    H, d, dv = DIFF_HEADS, DIFF_HEAD_DIM, DIFF_V_DIM
    q = apply_rope((x @ w_q).reshape(B, S, 2 * H, d), cos, sin) * (d ** -0.5)
    q = q.reshape(B, S, H, 2, d)
    lam_f = lam.astype(jnp.float32)
    lam_full = (jnp.exp(jnp.sum(lam_f[0] * lam_f[1])) - jnp.exp(jnp.sum(lam_f[2] * lam_f[3]))
                + lambda_init)
    blocks = []
    for i in range(S // Q_BLOCK):
        lo, hi = i * Q_BLOCK, (i + 1) * Q_BLOCK
        qb = q[:, lo:hi]
        kb = k_sh[:, :hi]
        vb = v_sh[:, :hi]
        s = jnp.einsum('bqhcd,bkhcd->bhcqk', qb, kb).astype(jnp.float32)
        mask = (lo + jnp.arange(Q_BLOCK))[:, None] >= jnp.arange(hi)[None, :]
        s = jnp.where(mask, s, -jnp.inf)
        prob = jax.nn.softmax(s, axis=-1)
        a = prob[:, :, 0] - lam_full * prob[:, :, 1]
        blocks.append(jnp.einsum('bhqk,bkhe->bqhe', a.astype(vb.dtype), vb))
    o = jnp.concatenate(blocks, axis=1)
    o = rms_norm(o, subln_g) * (1.0 - lambda_init)
    return o.reshape(B, S, H * dv) @ w_out


def swiglu(t, w13, w2):
    a, b = jnp.split(t @ w13, 2, axis=-1)
    return (jax.nn.silu(a) * b) @ w2


def moe_swiglu(x, w_router, w13, w2):
    B, S, D = x.shape
    t = x.reshape(B * S, D)
    logits = (t @ w_router).astype(jnp.float32)
    top_val, top_idx = lax.top_k(logits, TOP_K)
    top_w = jax.nn.softmax(top_val, axis=-1)
    gates = jnp.sum(jax.nn.one_hot(top_idx, N_EXPERTS, dtype=jnp.float32) * top_w[..., None], axis=1)
    gates = gates.astype(t.dtype)
    out = jnp.zeros_like(t)
    for e in range(N_EXPERTS):
        out = out + gates[:, e:e + 1] * swiglu(t, w13[e], w2[e])
    return out.reshape(B, S, D)


def setup_inputs(seed: int = 0) -> dict:
    key = jax.random.key(seed)
    ks = jax.random.split(key, 20)
    f32 = jnp.float32
    D = D_MODEL
    beta = DEEPNORM_BETA

    def nrm(k, shape, fan_in, scale=1.0):
        return jax.random.normal(k, shape, f32) * (scale * fan_in ** -0.5)

    x = jax.random.normal(ks[0], (BATCH, SEQ, D), f32)
    p = jax.random.normal(ks[1], (DEPTH, BATCH, SEQ, PLE_DIM), f32)
    qk_w = RET_HEADS * RET_QK_DIM
    v_w = RET_HEADS * RET_V_DIM
    col_scale = jnp.concatenate([jnp.full((2 * qk_w,), 1.0, f32), jnp.full((v_w,), beta, f32),
                                 jnp.full((v_w,), 1.0, f32)])
    ret_w_in = nrm(ks[2], (N_A_LAYERS, D, 2 * qk_w + 2 * v_w), D) * col_scale
    ret_w_out = nrm(ks[3], (N_A_LAYERS, v_w, D), v_w, beta)
    kd = 2 * DIFF_HEADS * DIFF_HEAD_DIM
    vd = DIFF_HEADS * DIFF_V_DIM
    kv_scale = jnp.concatenate([jnp.full((kd,), 1.0, f32), jnp.full((vd,), beta, f32)])
    kv_w = nrm(ks[4], (D, kd + vd), D) * kv_scale
    diff_w_q = nrm(ks[5], (N_B_LAYERS, D, kd), D)
    diff_w_out = nrm(ks[6], (N_B_LAYERS, vd, D), vd, beta)
    diff_lambda = 0.1 * jax.random.normal(ks[7], (N_B_LAYERS, 4, DIFF_HEAD_DIM), f32)
    diff_subln_g = 1.0 + 0.02 * jax.random.normal(ks[8], (N_B_LAYERS, DIFF_V_DIM), f32)
    ffn_w13 = nrm(ks[9], (N_DENSE, D, 2 * D_FF), D, beta)
    ffn_w2 = nrm(ks[10], (N_DENSE, D_FF, D), D_FF, beta)
    moe_router = nrm(ks[11], (N_MOE, D, N_EXPERTS), D)
    moe_w13 = nrm(ks[12], (N_MOE, N_EXPERTS, D, 2 * D_EXPERT), D, beta)
    moe_w2 = nrm(ks[13], (N_MOE, N_EXPERTS, D_EXPERT, D), D_EXPERT, beta)
    ple_w_gate = nrm(ks[14], (DEPTH, D, D), D)
    ple_w_proj = nrm(ks[15], (DEPTH, PLE_DIM, D), PLE_DIM, beta)
    ln_g = 1.0 + 0.02 * jax.random.normal(ks[16], (DEPTH, 3, D), f32)
    ln_b = 0.02 * jax.random.normal(ks[17], (DEPTH, 3, D), f32)
    return {"x": x, "p": p, "ret_w_in": ret_w_in, "ret_w_out": ret_w_out, "kv_w": kv_w,
            "diff_w_q": diff_w_q, "diff_w_out": diff_w_out, "diff_lambda": diff_lambda,
            "diff_subln_g": diff_subln_g, "ffn_w13": ffn_w13, "ffn_w2": ffn_w2,
            "moe_router": moe_router, "moe_w13": moe_w13, "moe_w2": moe_w2,
            "ple_w_gate": ple_w_gate, "ple_w_proj": ple_w_proj, "ln_g": ln_g, "ln_b": ln_b}


def reference(x, p, ret_w_in, ret_w_out, kv_w, diff_w_q, diff_w_out, diff_lambda, diff_subln_g,
              ffn_w13, ffn_w2, moe_router, moe_w13, moe_w2, ple_w_gate, ple_w_proj, ln_g, ln_b):
    S = x.shape[1]
    cos_r, sin_r = rope_tables(S, RET_QK_DIM)
    cos_d, sin_d = rope_tables(S, DIFF_HEAD_DIM)
    k_sh, v_sh = None, None
    for i in range(DEPTH):
        if i < N_A_LAYERS:
            mix = retention(x, ret_w_in[i], ret_w_out[i], cos_r, sin_r)
        else:
            j = i - N_A_LAYERS
            lambda_init = 0.8 - 0.6 * math.exp(-0.3 * i)
            mix = diff_attention(x, diff_w_q[j], diff_w_out[j], diff_lambda[j], diff_subln_g[j],
                                 k_sh, v_sh, cos_d, sin_d, lambda_init)
        x = layer_norm(DEEPNORM_ALPHA * x + mix, ln_g[i, 0], ln_b[i, 0])
        if i % 2 == 0:
            ff = swiglu(x, ffn_w13[i // 2], ffn_w2[i // 2])
        else:
            ff = moe_swiglu(x, moe_router[i // 2], moe_w13[i // 2], moe_w2[i // 2])
        x = layer_norm(DEEPNORM_ALPHA * x + ff, ln_g[i, 1], ln_b[i, 1])
        ple = jax.nn.sigmoid(x @ ple_w_gate[i]) * (p[i] @ ple_w_proj[i])
        x = layer_norm(DEEPNORM_ALPHA * x + ple, ln_g[i, 2], ln_b[i, 2])
        if i == N_A_LAYERS - 1:
            k_sh, v_sh = shared_kv(x, kv_w, cos_d, sin_d)
    return x
```

```python
import functools
import math

import jax
import jax.numpy as jnp
from jax import lax
from jax.experimental import pallas as pl
from jax.experimental.pallas import tpu as pltpu

F32 = jnp.float32
BF16 = jnp.bfloat16

D_MODEL = 4096
DEPTH = 4
N_A_LAYERS = DEPTH // 2
RET_HEADS = 16
RET_QK_DIM = D_MODEL // RET_HEADS
RET_V_DIM = 2 * D_MODEL // RET_HEADS
RET_CHUNK = 128
DIFF_HEADS = 16
DIFF_HEAD_DIM = D_MODEL // (2 * DIFF_HEADS)
DIFF_V_DIM = 2 * DIFF_HEAD_DIM
ROPE_THETA = 10000.0
D_FF = 2 * D_MODEL
N_EXPERTS = 8
TOP_K = 2
D_EXPERT = D_MODEL // 4
PLE_DIM = 256
LN_EPS = 1e-5
DEEPNORM_ALPHA = (2.0 * DEPTH) ** 0.25

V7X_LANES = 128
V7X_VMEM_BYTES = 64 * 1024 * 1024
V7X_VMEM_CAP = V7X_VMEM_BYTES - 6 * 1024 * 1024

MM_BM = 1024
MM_BM_WIDE_K = 512
MM_BN = 512
LN_ROWS = 256
RET_ROWS = 512
ATT_TILE = 512


def _nbytes(shape, dtype):
    return math.prod(shape) * jnp.dtype(dtype).itemsize


def _compiler_params(semantics, block_bytes, temp_bytes):
    need = 2 * block_bytes + temp_bytes + (2 << 20)
    return pltpu.CompilerParams(
        dimension_semantics=semantics,
        vmem_limit_bytes=int(min(max(need, 16 << 20), V7X_VMEM_CAP)),
    )


def _w_spec(w, layer, k, bn, col_block_off=0):
    if w.ndim == 3:
        return pl.BlockSpec((None, k, bn), lambda i, j: (layer, 0, j + col_block_off))
    return pl.BlockSpec((k, bn), lambda i, j: (0, j + col_block_off))


def _mm_plain_body(a_ref, w_ref, o_ref):
    o_ref[...] = jnp.dot(a_ref[...], w_ref[...], preferred_element_type=F32).astype(o_ref.dtype)


def matmul(a, w, *, layer=None, col_off=0, n_out, out_dtype, bm, bn=MM_BN):
    m, k = a.shape
    bm = min(bm, m)
    assert m % bm == 0 and n_out % bn == 0 and col_off % bn == 0
    blocks = _nbytes((bm, k), a.dtype) + _nbytes((k, bn), w.dtype) + _nbytes((bm, bn), out_dtype)
    return pl.pallas_call(
        _mm_plain_body,
        grid=(m // bm, n_out // bn),
        in_specs=[pl.BlockSpec((bm, k), lambda i, j: (i, 0)), _w_spec(w, layer, k, bn, col_off // bn)],
        out_specs=pl.BlockSpec((bm, bn), lambda i, j: (i, j)),
        out_shape=jax.ShapeDtypeStruct((m, n_out), out_dtype),
        compiler_params=_compiler_params(("parallel", "arbitrary"), blocks, 2 * _nbytes((bm, bn), F32)),
        name="matmul",
    )(a, w)


def _mm_rope_body(a_ref, w_ref, cos_ref, sin_ref, o_ref, *, head_dim, split_block, scale_lo, scale_hi):
    acc = jnp.dot(a_ref[...], w_ref[...], preferred_element_type=F32)
    cos = cos_ref[...]
    sin = sin_ref[...]
    if scale_lo == scale_hi:
        scale = scale_lo
    else:
        scale = jnp.where(pl.program_id(1) < split_block, scale_lo, scale_hi)
    for h in range(acc.shape[1] // head_dim):
        cols = slice(h * head_dim, (h + 1) * head_dim)
        xb = acc[:, cols]
        rot = pltpu.roll(xb, head_dim // 2, axis=1)
        o_ref[:, cols] = ((xb * cos + rot * sin) * scale).astype(o_ref.dtype)


def matmul_rope(a, w, cos_full, sin_signed, *, layer=None, col_off=0, n_out, head_dim,
                split_col=0, scale_lo=1.0, scale_hi=1.0, bm=MM_BM, bn=MM_BN):
    m, k = a.shape
    seq = cos_full.shape[0]
    bm = min(bm, seq)
    assert m % bm == 0 and seq % bm == 0 and n_out % bn == 0 and bn % head_dim == 0
    assert col_off % bn == 0 and split_col % bn == 0
    pos_tiles = seq // bm
    blocks = (_nbytes((bm, k), a.dtype) + _nbytes((k, bn), w.dtype) + _nbytes((bm, bn), BF16)
              + 2 * _nbytes((bm, head_dim), F32))
    body = functools.partial(_mm_rope_body, head_dim=head_dim, split_block=split_col // bn,
                             scale_lo=scale_lo, scale_hi=scale_hi)
    return pl.pallas_call(
        body,
        grid=(m // bm, n_out // bn),
        in_specs=[pl.BlockSpec((bm, k), lambda i, j: (i, 0)),
                  _w_spec(w, layer, k, bn, col_off // bn),
                  pl.BlockSpec((bm, head_dim), lambda i, j: (i % pos_tiles, 0)),
                  pl.BlockSpec((bm, head_dim), lambda i, j: (i % pos_tiles, 0))],
        out_specs=pl.BlockSpec((bm, bn), lambda i, j: (i, j)),
        out_shape=jax.ShapeDtypeStruct((m, n_out), BF16),
        compiler_params=_compiler_params(("parallel", "arbitrary"), blocks, 3 * _nbytes((bm, bn), F32)),
        name="matmul_rope",
    )(a, w, cos_full, sin_signed)


def _mm_swiglu_body(a_ref, wa_ref, wb_ref, o_ref):
    a = a_ref[...]
    ga = jnp.dot(a, wa_ref[...], preferred_element_type=F32)
    gb = jnp.dot(a, wb_ref[...], preferred_element_type=F32)
    o_ref[...] = (jax.nn.silu(ga) * gb).astype(o_ref.dtype)


def matmul_swiglu(a, w13, *, layer, bm=MM_BM, bn=MM_BN):
    m, k = a.shape
    f = w13.shape[-1] // 2
    bm = min(bm, m)
    assert m % bm == 0 and f % bn == 0
    blocks = _nbytes((bm, k), a.dtype) + 2 * _nbytes((k, bn), w13.dtype) + _nbytes((bm, bn), BF16)
    return pl.pallas_call(
        _mm_swiglu_body,
        grid=(m // bm, f // bn),
        in_specs=[pl.BlockSpec((bm, k), lambda i, j: (i, 0)),
                  _w_spec(w13, layer, k, bn, 0),
                  _w_spec(w13, layer, k, bn, f // bn)],
        out_specs=pl.BlockSpec((bm, bn), lambda i, j: (i, j)),
        out_shape=jax.ShapeDtypeStruct((m, f), BF16),
        compiler_params=_compiler_params(("parallel", "arbitrary"), blocks, 4 * _nbytes((bm, bn), F32)),
        name="matmul_swiglu",
    )(a, w13, w13)


def _mm_moe_swiglu_body(a_ref, wa_ref, wb_ref, gates_ref, o_ref, *, blocks_per_expert):
    a = a_ref[...]
    ga = jnp.dot(a, wa_ref[...], preferred_element_type=F32)
    gb = jnp.dot(a, wb_ref[...], preferred_element_type=F32)
    expert = pl.program_id(1) // blocks_per_expert
    gates = gates_ref[...]
    lane = lax.broadcasted_iota(jnp.int32, gates.shape, 1)
    gate = jnp.sum(jnp.where(lane == expert, gates, 0.0), axis=1, keepdims=True)
    o_ref[...] = (gate * (jax.nn.silu(ga) * gb)).astype(o_ref.dtype)


def matmul_moe_swiglu(a, w13, gates, *, layer, bm=MM_BM, bn=MM_BN):
    m, k = a.shape
    n_exp, f = w13.shape[1], w13.shape[-1] // 2
    bm = min(bm, m)
    assert m % bm == 0 and f % bn == 0
    bpe = f // bn
    blocks = (_nbytes((bm, k), a.dtype) + 2 * _nbytes((k, bn), w13.dtype) + _nbytes((bm, bn), BF16)
              + _nbytes((bm, V7X_LANES), F32))
    return pl.pallas_call(
        functools.partial(_mm_moe_swiglu_body, blocks_per_expert=bpe),
        grid=(m // bm, n_exp * bpe),
        in_specs=[pl.BlockSpec((bm, k), lambda i, j: (i, 0)),
                  pl.BlockSpec((None, None, k, bn), lambda i, j: (layer, j // bpe, 0, j % bpe)),
                  pl.BlockSpec((None, None, k, bn), lambda i, j: (layer, j // bpe, 0, j % bpe + bpe)),
                  pl.BlockSpec((bm, V7X_LANES), lambda i, j: (i, 0))],
        out_specs=pl.BlockSpec((bm, bn), lambda i, j: (i, j)),
        out_shape=jax.ShapeDtypeStruct((m, n_exp * f), BF16),
        compiler_params=_compiler_params(("parallel", "arbitrary"), blocks, 4 * _nbytes((bm, bn), F32)),
        name="matmul_moe_swiglu",
    )(a, w13, w13, gates)


def _mm_ple_body(a_ref, wg_ref, p_ref, wp_ref, o_ref):
    gate = jnp.dot(a_ref[...], wg_ref[...], preferred_element_type=F32)
    proj = jnp.dot(p_ref[...].astype(BF16), wp_ref[...], preferred_element_type=F32)
    o_ref[...] = jax.nn.sigmoid(gate) * proj


def matmul_ple(a, w_gate, p, w_proj, *, layer, bm=MM_BM, bn=MM_BN):
    m, k = a.shape
    n = w_gate.shape[-1]
    kp = p.shape[-1]
    bm = min(bm, m)
    assert m % bm == 0 and n % bn == 0
    blocks = (_nbytes((bm, k), a.dtype) + _nbytes((k, bn), w_gate.dtype) + _nbytes((bm, kp), p.dtype)
              + _nbytes((kp, bn), w_proj.dtype) + _nbytes((bm, bn), F32))
    return pl.pallas_call(
        _mm_ple_body,
        grid=(m // bm, n // bn),
        in_specs=[pl.BlockSpec((bm, k), lambda i, j: (i, 0)),
                  _w_spec(w_gate, layer, k, bn),
                  pl.BlockSpec((None, bm, kp), lambda i, j: (layer, i, 0)),
                  _w_spec(w_proj, layer, kp, bn)],
        out_specs=pl.BlockSpec((bm, bn), lambda i, j: (i, j)),
        out_shape=jax.ShapeDtypeStruct((m, n), F32),
        compiler_params=_compiler_params(("parallel", "arbitrary"), blocks, 4 * _nbytes((bm, bn), F32)),
        name="matmul_ple",
    )(a, w_gate, p, w_proj)


def _ln_body(x_ref, u_ref, g_ref, b_ref, o_ref, ob_ref):
    y = DEEPNORM_ALPHA * x_ref[...] + u_ref[...]
    mu = jnp.mean(y, axis=-1, keepdims=True)
    d = y - mu
    var = jnp.mean(d * d, axis=-1, keepdims=True)
    out = d * lax.rsqrt(var + LN_EPS) * g_ref[...] + b_ref[...]
    o_ref[...] = out
    ob_ref[...] = out.astype(BF16)


def residual_layer_norm(x, update, ln_g, ln_b, *, row, rows=LN_ROWS):
    m, d = x.shape
    rows = min(rows, m)
    assert m % rows == 0
    blocks = 3 * _nbytes((rows, d), F32) + _nbytes((rows, d), BF16) + 2 * _nbytes((1, d), F32)
    tile = pl.BlockSpec((rows, d), lambda i: (i, 0))
    vec = pl.BlockSpec((None, 1, d), lambda i: (row, 0, 0))
    return pl.pallas_call(
        _ln_body,
        grid=(m // rows,),
        in_specs=[tile, tile, vec, vec],
        out_specs=[tile, tile],
        out_shape=[jax.ShapeDtypeStruct((m, d), F32), jax.ShapeDtypeStruct((m, d), BF16)],
        compiler_params=_compiler_params(("parallel",), blocks, 3 * _nbytes((rows, d), F32)),
        name="residual_layer_norm",
    )(x, update, ln_g, ln_b)


def _retention_body(q_ref, k_ref, v_ref, g_ref, dmask_ref, qdec_ref, kdec_ref, cdec_ref, o_ref, state_ref,
                    *, n_chunks, chunk):
    @pl.when(pl.program_id(2) == 0)
    def _():
        state_ref[...] = jnp.zeros_like(state_ref)

    dmask = dmask_ref[...]
    qdec = qdec_ref[...]
    kdec = kdec_ref[...]
    cdec = cdec_ref[...]
    for c in range(n_chunks):
        rows = pl.ds(c * chunk, chunk)
        q = q_ref[rows, :]
        k = k_ref[rows, :]
        v = v_ref[rows, :]
        state = state_ref[...]
        scores = lax.dot_general(q, k, (((1,), (1,)), ((), ())), preferred_element_type=F32) * dmask
        inner = jnp.dot(scores.astype(BF16), v, preferred_element_type=F32)
        cross = jnp.dot((q.astype(F32) * qdec).astype(BF16), state.astype(BF16), preferred_element_type=F32)
        k_scaled = (k.astype(F32) * kdec).astype(BF16)
        update = lax.dot_general(k_scaled, v, (((0,), (0,)), ((), ())), preferred_element_type=F32)
        state_ref[...] = state * cdec + update
        out = inner + cross
        mu = jnp.mean(out, axis=-1, keepdims=True)
        d = out - mu
        var = jnp.mean(d * d, axis=-1, keepdims=True)
        normed = d * lax.rsqrt(var + LN_EPS)
        gate = g_ref[rows, :].astype(F32)
        o_ref[rows, :] = (jax.nn.silu(gate) * normed).astype(o_ref.dtype)


def retention_tables():
    h, c = RET_HEADS, RET_CHUNK
    log_gamma = jnp.log1p(-jnp.exp2(-5.0 - jnp.arange(h, dtype=F32)))
    pos = jnp.arange(c, dtype=F32)
    rel = pos[:, None] - pos[None, :]
    dmask = jnp.where(rel >= 0, jnp.exp(log_gamma[:, None, None] * jnp.maximum(rel, 0.0)), 0.0)
    qdec = jnp.exp(log_gamma[:, None] * (pos + 1.0))[:, :, None]
    kdec = jnp.exp(log_gamma[:, None] * (c - 1.0 - pos))[:, :, None]
    cdec = jnp.exp(log_gamma * c)[:, None, None]
    return (dmask,
            jnp.broadcast_to(qdec, (h, c, RET_QK_DIM)),
            jnp.broadcast_to(kdec, (h, c, RET_QK_DIM)),
            jnp.broadcast_to(cdec, (h, 1, RET_V_DIM)))


def retention(qk, vg, tables, *, batch, seq, rows=RET_ROWS):
    h, dk, dv, c = RET_HEADS, RET_QK_DIM, RET_V_DIM, RET_CHUNK
    rows = min(rows, seq)
    assert seq % rows == 0 and rows % c == 0
    steps = seq // rows
    dmask, qdec, kdec, cdec = tables
    row_tile = lambda b, hh, l: b * steps + l
    blocks = (2 * _nbytes((rows, dk), BF16) + 3 * _nbytes((rows, dv), BF16) + _nbytes((c, c), F32)
              + 2 * _nbytes((c, dk), F32) + _nbytes((8, dv), F32))
    return pl.pallas_call(
        functools.partial(_retention_body, n_chunks=rows // c, chunk=c),
        grid=(batch, h, steps),
        in_specs=[pl.BlockSpec((rows, dk), lambda b, hh, l: (row_tile(b, hh, l), hh)),
                  pl.BlockSpec((rows, dk), lambda b, hh, l: (row_tile(b, hh, l), h + hh)),
                  pl.BlockSpec((rows, dv), lambda b, hh, l: (row_tile(b, hh, l), hh)),
                  pl.BlockSpec((rows, dv), lambda b, hh, l: (row_tile(b, hh, l), h + hh)),
                  pl.BlockSpec((None, c, c), lambda b, hh, l: (hh, 0, 0)),
                  pl.BlockSpec((None, c, dk), lambda b, hh, l: (hh, 0, 0)),
                  pl.BlockSpec((None, c, dk), lambda b, hh, l: (hh, 0, 0)),
                  pl.BlockSpec((None, 1, dv), lambda b, hh, l: (hh, 0, 0))],
        out_specs=pl.BlockSpec((rows, dv), lambda b, hh, l: (row_tile(b, hh, l), hh)),
        out_shape=jax.ShapeDtypeStruct((batch * seq, h * dv), BF16),
        scratch_shapes=[pltpu.VMEM((dk, dv), F32)],
        compiler_params=_compiler_params(("parallel", "parallel", "arbitrary"), blocks,
                                         _nbytes((dk, dv), F32) * 4 + _nbytes((c, dv), F32) * 8),
        name="retention",
    )(qk, qk, vg, vg, dmask, qdec, kdec, cdec)


def _diff_attention_body(qt_ref, kt_ref, q_ref, k_ref, v_ref, lam_ref, g_ref, o_ref, m_ref, l_ref, acc_ref,
                         *, tile, lambda_init):
    pair = pl.program_id(2)
    qi = qt_ref[pair]
    ki = kt_ref[pair]
    d = DIFF_HEAD_DIM

    @pl.when(ki == 0)
    def _():
        m_ref[...] = jnp.full_like(m_ref, -jnp.inf)
        l_ref[...] = jnp.zeros_like(l_ref)
        acc_ref[...] = jnp.zeros_like(acc_ref)

    q = q_ref[...]
    k = k_ref[...]
    v = v_ref[...]
    row = qi * tile + lax.broadcasted_iota(jnp.int32, (tile, tile), 0)
    col = ki * tile + lax.broadcasted_iota(jnp.int32, (tile, tile), 1)
    causal = row >= col
    for c in range(2):
        s = lax.dot_general(q[:, c * d:(c + 1) * d], k[:, c * d:(c + 1) * d], (((1,), (1,)), ((), ())),
                            preferred_element_type=F32)
        s = jnp.where(causal, s, -jnp.inf)
        m_prev = m_ref[c]
        m_new = jnp.maximum(m_prev, jnp.max(s, axis=1, keepdims=True))
        alpha = jnp.exp(m_prev - m_new)
        p = jnp.exp(s - m_new)
        l_ref[c] = alpha * l_ref[c] + jnp.sum(p, axis=1, keepdims=True)
        acc_ref[c] = alpha * acc_ref[c] + jnp.dot(p.astype(BF16), v, preferred_element_type=F32)
        m_ref[c] = m_new

    @pl.when(ki == qi)
    def _():
        lam = lam_ref[...]
        lam_full = (jnp.exp(jnp.sum(lam[0:1] * lam[1:2], axis=1, keepdims=True))
                    - jnp.exp(jnp.sum(lam[2:3] * lam[3:4], axis=1, keepdims=True)) + lambda_init)
        out = acc_ref[0] / l_ref[0] - lam_full * (acc_ref[1] / l_ref[1])
        y = out * lax.rsqrt(jnp.mean(out * out, axis=-1, keepdims=True) + LN_EPS)
        o_ref[...] = ((y * g_ref[...]) * (1.0 - lambda_init)).astype(o_ref.dtype)


def diff_attention(q, k, v, lam, subln_g, *, layer, batch, seq, lambda_init, tile=ATT_TILE):
    h, dv = DIFF_HEADS, DIFF_V_DIM
    tile = min(tile, seq)
    assert seq % tile == 0
    nq = seq // tile
    pairs = [(qi, ki) for qi in range(nq) for ki in range(qi + 1)]
    q_tiles = jnp.asarray([qk_[0] for qk_ in pairs], jnp.int32)
    k_tiles = jnp.asarray([qk_[1] for qk_ in pairs], jnp.int32)
    blocks = 4 * _nbytes((tile, dv), BF16) + _nbytes((8, V7X_LANES), F32) + _nbytes((8, dv), F32)
    grid_spec = pltpu.PrefetchScalarGridSpec(
        num_scalar_prefetch=2,
        grid=(batch, h, len(pairs)),
        in_specs=[pl.BlockSpec((tile, dv), lambda b, hh, p, qt, kt: (b * nq + qt[p], hh)),
                  pl.BlockSpec((tile, dv), lambda b, hh, p, qt, kt: (b * nq + kt[p], hh)),
                  pl.BlockSpec((tile, dv), lambda b, hh, p, qt, kt: (b * nq + kt[p], hh)),
                  pl.BlockSpec((None, 4, DIFF_HEAD_DIM), lambda b, hh, p, qt, kt: (layer, 0, 0)),
                  pl.BlockSpec((None, 1, dv), lambda b, hh, p, qt, kt: (layer, 0, 0))],
        out_specs=pl.BlockSpec((tile, dv), lambda b, hh, p, qt, kt: (b * nq + qt[p], hh)),
        scratch_shapes=[pltpu.VMEM((2, tile, 1), F32), pltpu.VMEM((2, tile, 1), F32),
                        pltpu.VMEM((2, tile, dv), F32)],
    )
    scratch = 2 * 2 * _nbytes((tile, V7X_LANES), F32) + 2 * _nbytes((tile, dv), F32)
    return pl.pallas_call(
        functools.partial(_diff_attention_body, tile=tile, lambda_init=lambda_init),
        grid_spec=grid_spec,
        out_shape=jax.ShapeDtypeStruct((batch * seq, h * dv), BF16),
        compiler_params=_compiler_params(("parallel", "parallel", "arbitrary"), blocks,
                                         scratch + 8 * _nbytes((tile, tile), F32)),
        name="diff_attention",
    )(q_tiles, k_tiles, q, k, v, lam, subln_g)


def _router_body(a_ref, w_ref, o_ref):
    logits = jnp.dot(a_ref[...], w_ref[...], preferred_element_type=F32)
    lane = lax.broadcasted_iota(jnp.int32, logits.shape, 1).astype(F32)
    logits = jnp.where(lane < N_EXPERTS, logits, -jnp.inf)
    top1 = jnp.max(logits, axis=1, keepdims=True)
    idx1 = jnp.min(jnp.where(logits == top1, lane, float(V7X_LANES)), axis=1, keepdims=True)
    rest = jnp.where(lane == idx1, -jnp.inf, logits)
    top2 = jnp.max(rest, axis=1, keepdims=True)
    idx2 = jnp.min(jnp.where(rest == top2, lane, float(V7X_LANES)), axis=1, keepdims=True)
    e = jnp.exp(top2 - top1)
    w1 = 1.0 / (1.0 + e)
    w2 = e / (1.0 + e)
    o_ref[...] = jnp.where(lane == idx1, w1, 0.0) + jnp.where(lane == idx2, w2, 0.0)


def router_gates(a, w_router_padded, *, layer, bm=MM_BM):
    m, k = a.shape
    bm = min(bm, m)
    assert m % bm == 0
    blocks = _nbytes((bm, k), a.dtype) + _nbytes((k, V7X_LANES), BF16) + _nbytes((bm, V7X_LANES), F32)
    return pl.pallas_call(
        _router_body,
        grid=(m // bm,),
        in_specs=[pl.BlockSpec((bm, k), lambda i: (i, 0)),
                  pl.BlockSpec((None, k, V7X_LANES), lambda i: (layer, 0, 0))],
        out_specs=pl.BlockSpec((bm, V7X_LANES), lambda i: (i, 0)),
        out_shape=jax.ShapeDtypeStruct((m, V7X_LANES), F32),
        compiler_params=_compiler_params(("parallel",), blocks, 8 * _nbytes((bm, V7X_LANES), F32)),
        name="router_gates",
    )(a, w_router_padded)


def _rope_tables(seq, dim):
    inv_freq = 1.0 / (ROPE_THETA ** (jnp.arange(0, dim, 2, dtype=F32) / dim))
    ang = jnp.arange(seq, dtype=F32)[:, None] * inv_freq[None, :]
    cos, sin = jnp.cos(ang), jnp.sin(ang)
    return jnp.concatenate([cos, cos], axis=-1), jnp.concatenate([-sin, sin], axis=-1)


def kernel(x, p, ret_w_in, ret_w_out, kv_w, diff_w_q, diff_w_out, diff_lambda, diff_subln_g, ffn_w13, ffn_w2,
           moe_router, moe_w13, moe_w2, ple_w_gate, ple_w_proj, ln_g, ln_b):
    batch, seq, d = x.shape
    assert d == D_MODEL
    tokens = batch * seq
    cos_r, sin_r = _rope_tables(seq, RET_QK_DIM)
    cos_d, sin_d = _rope_tables(seq, DIFF_HEAD_DIM)
    ret_tabs = retention_tables()

    bf = lambda w: w.astype(BF16)
    ret_w_in, ret_w_out, kv_w, diff_w_q, diff_w_out = map(bf, (ret_w_in, ret_w_out, kv_w, diff_w_q, diff_w_out))
    ffn_w13, ffn_w2, moe_w13, ple_w_gate, ple_w_proj = map(bf, (ffn_w13, ffn_w2, moe_w13, ple_w_gate, ple_w_proj))
    moe_w2 = bf(moe_w2).reshape(moe_w2.shape[0], N_EXPERTS * D_EXPERT, D_MODEL)
    router_w = jnp.pad(bf(moe_router), ((0, 0), (0, 0), (0, V7X_LANES - N_EXPERTS)))
    ln_g = ln_g.reshape(DEPTH * 3, 1, D_MODEL)
    ln_b = ln_b.reshape(DEPTH * 3, 1, D_MODEL)
    subln_g = diff_subln_g.reshape(-1, 1, DIFF_V_DIM)
    p = p.reshape(DEPTH, tokens, PLE_DIM)

    x = x.reshape(tokens, D_MODEL)
    xb = x.astype(BF16)
    qk_w = RET_HEADS * RET_QK_DIM
    v_w = RET_HEADS * RET_V_DIM
    kd = 2 * DIFF_HEADS * DIFF_HEAD_DIM
    vd = DIFF_HEADS * DIFF_V_DIM
    k_sh = v_sh = None
    for i in range(DEPTH):
        if i < N_A_LAYERS:
            qk = matmul_rope(xb, ret_w_in, cos_r, sin_r, layer=i, col_off=0, n_out=2 * qk_w,
                             head_dim=RET_QK_DIM, split_col=qk_w, scale_lo=1.0, scale_hi=RET_QK_DIM ** -0.5)
            vg = matmul(xb, ret_w_in, layer=i, col_off=2 * qk_w, n_out=2 * v_w, out_dtype=BF16, bm=MM_BM)
            gated = retention(qk, vg, ret_tabs, batch=batch, seq=seq)
            mix = matmul(gated, ret_w_out, layer=i, n_out=D_MODEL, out_dtype=F32, bm=MM_BM_WIDE_K)
        else:
            j = i - N_A_LAYERS
            lambda_init = 0.8 - 0.6 * math.exp(-0.3 * i)
            q = matmul_rope(xb, diff_w_q, cos_d, sin_d, layer=j, n_out=kd, head_dim=DIFF_HEAD_DIM,
                            scale_lo=DIFF_HEAD_DIM ** -0.5, scale_hi=DIFF_HEAD_DIM ** -0.5)
            att = diff_attention(q, k_sh, v_sh, diff_lambda, subln_g, layer=j, batch=batch, seq=seq,
                                 lambda_init=lambda_init)
            mix = matmul(att, diff_w_out, layer=j, n_out=D_MODEL, out_dtype=F32, bm=MM_BM)
        x, xb = residual_layer_norm(x, mix, ln_g, ln_b, row=3 * i)
        if i % 2 == 0:
            hidden = matmul_swiglu(xb, ffn_w13, layer=i // 2)
            ff = matmul(hidden, ffn_w2, layer=i // 2, n_out=D_MODEL, out_dtype=F32, bm=MM_BM_WIDE_K)
        else:
            gates = router_gates(xb, router_w, layer=i // 2)
            hidden = matmul_moe_swiglu(xb, moe_w13, gates, layer=i // 2)
            ff = matmul(hidden, moe_w2, layer=i // 2, n_out=D_MODEL, out_dtype=F32, bm=MM_BM_WIDE_K)
        x, xb = residual_layer_norm(x, ff, ln_g, ln_b, row=3 * i + 1)
        ple = matmul_ple(xb, ple_w_gate, p, ple_w_proj, layer=i)
        x, xb = residual_layer_norm(x, ple, ln_g, ln_b, row=3 * i + 2)
        if i == N_A_LAYERS - 1:
            k_sh = matmul_rope(xb, kv_w, cos_d, sin_d, col_off=0, n_out=kd, head_dim=DIFF_HEAD_DIM)
            v_sh = matmul(xb, kv_w, col_off=kd, n_out=vd, out_dtype=BF16, bm=MM_BM)
    return x.reshape(batch, seq, D_MODEL)
```

```python
import functools
import math

import jax
import jax.numpy as jnp
from jax import lax
from jax.experimental import pallas as pl
from jax.experimental.pallas import tpu as pltpu

F32 = jnp.float32
BF16 = jnp.bfloat16

D_MODEL = 4096
DEPTH = 4
N_A_LAYERS = DEPTH // 2
RET_HEADS = 16
RET_QK_DIM = D_MODEL // RET_HEADS
RET_V_DIM = 2 * D_MODEL // RET_HEADS
RET_CHUNK = 128
DIFF_HEADS = 16
DIFF_HEAD_DIM = D_MODEL // (2 * DIFF_HEADS)
DIFF_V_DIM = 2 * DIFF_HEAD_DIM
ROPE_THETA = 10000.0
D_FF = 2 * D_MODEL
N_EXPERTS = 8
TOP_K = 2
D_EXPERT = D_MODEL // 4
PLE_DIM = 256
LN_EPS = 1e-5
DEEPNORM_ALPHA = (2.0 * DEPTH) ** 0.25

V7X_LANES = 128
V7X_VMEM_BYTES = 64 * 1024 * 1024
V7X_VMEM_CAP = V7X_VMEM_BYTES - 6 * 1024 * 1024

MM_BM = 1024
MM_BM_WIDE_K = 512
MM_BN = 512
MM_BN_WIDE = 1024
LN_ROWS = 256
RET_ROWS = 512
ATT_TILE = 512
ATT_STRIP = 64
LOG2_E = math.log2(math.e)


def _nbytes(shape, dtype):
    return math.prod(shape) * jnp.dtype(dtype).itemsize


def _compiler_params(semantics, block_bytes, temp_bytes):
    need = 2 * block_bytes + temp_bytes + (2 << 20)
    return pltpu.CompilerParams(
        dimension_semantics=semantics,
        vmem_limit_bytes=int(min(max(need, 16 << 20), V7X_VMEM_CAP)),
    )


def _w_spec(w, layer, k, bn, col_block_off=0):
    if w.ndim == 3:
        return pl.BlockSpec((None, k, bn), lambda i, j: (layer, 0, j + col_block_off))
    return pl.BlockSpec((k, bn), lambda i, j: (0, j + col_block_off))


def _mm_plain_body(a_ref, w_ref, o_ref):
    o_ref[...] = jnp.dot(a_ref[...], w_ref[...], preferred_element_type=F32).astype(o_ref.dtype)


def matmul(a, w, *, layer=None, col_off=0, n_out, out_dtype):
    m, k = a.shape
    bm, bn = (MM_BM, MM_BN_WIDE) if k <= D_MODEL else (MM_BM_WIDE_K, MM_BN)
    bm = min(bm, m)
    assert m % bm == 0 and n_out % bn == 0 and col_off % bn == 0
    blocks = _nbytes((bm, k), a.dtype) + _nbytes((k, bn), w.dtype) + _nbytes((bm, bn), out_dtype)
    return pl.pallas_call(
        _mm_plain_body,
        grid=(m // bm, n_out // bn),
        in_specs=[pl.BlockSpec((bm, k), lambda i, j: (i, 0)), _w_spec(w, layer, k, bn, col_off // bn)],
        out_specs=pl.BlockSpec((bm, bn), lambda i, j: (i, j)),
        out_shape=jax.ShapeDtypeStruct((m, n_out), out_dtype),
        compiler_params=_compiler_params(("parallel", "arbitrary"), blocks, 2 * _nbytes((bm, bn), F32)),
        name="matmul",
    )(a, w)


def _mm_residual_body(a_ref, w_ref, x_ref, o_ref):
    o_ref[...] = DEEPNORM_ALPHA * x_ref[...] + jnp.dot(a_ref[...], w_ref[...], preferred_element_type=F32)


def matmul_residual(a, w, x, *, layer):
    m, k = a.shape
    n = w.shape[-1]
    bm, bn = (MM_BM, MM_BN) if k <= D_MODEL else (MM_BM_WIDE_K, MM_BN)
    bm = min(bm, m)
    assert m % bm == 0 and n % bn == 0 and x.shape == (m, n)
    blocks = _nbytes((bm, k), a.dtype) + _nbytes((k, bn), w.dtype) + 2 * _nbytes((bm, bn), F32)
    return pl.pallas_call(
        _mm_residual_body,
        grid=(m // bm, n // bn),
        in_specs=[pl.BlockSpec((bm, k), lambda i, j: (i, 0)), _w_spec(w, layer, k, bn),
                  pl.BlockSpec((bm, bn), lambda i, j: (i, j))],
        out_specs=pl.BlockSpec((bm, bn), lambda i, j: (i, j)),
        out_shape=jax.ShapeDtypeStruct((m, n), F32),
        compiler_params=_compiler_params(("parallel", "arbitrary"), blocks, 2 * _nbytes((bm, bn), F32)),
        name="matmul_residual",
    )(a, w, x)


def _mm_rope_body(a_ref, w_ref, cos_ref, sin_ref, o_ref, *, head_dim, split_block, scale_lo, scale_hi):
    acc = jnp.dot(a_ref[...], w_ref[...], preferred_element_type=F32)
    cos = cos_ref[...]
    sin = sin_ref[...]
    if scale_lo == scale_hi:
        scale = scale_lo
    else:
        scale = jnp.where(pl.program_id(1) < split_block, scale_lo, scale_hi)
    for h in range(acc.shape[1] // head_dim):
        cols = slice(h * head_dim, (h + 1) * head_dim)
        xb = acc[:, cols]
        rot = pltpu.roll(xb, head_dim // 2, axis=1)
        o_ref[:, cols] = ((xb * cos + rot * sin) * scale).astype(o_ref.dtype)


def matmul_rope(a, w, cos_full, sin_signed, *, layer=None, col_off=0, n_out, head_dim,
                split_col=0, scale_lo=1.0, scale_hi=1.0, bm=MM_BM, bn=MM_BN_WIDE):
    m, k = a.shape
    seq = cos_full.shape[0]
    bm = min(bm, seq)
    assert m % bm == 0 and seq % bm == 0 and n_out % bn == 0 and bn % head_dim == 0
    assert col_off % bn == 0 and split_col % bn == 0
    pos_tiles = seq // bm
    blocks = (_nbytes((bm, k), a.dtype) + _nbytes((k, bn), w.dtype) + _nbytes((bm, bn), BF16)
              + 2 * _nbytes((bm, head_dim), F32))
    body = functools.partial(_mm_rope_body, head_dim=head_dim, split_block=split_col // bn,
                             scale_lo=scale_lo, scale_hi=scale_hi)
    return pl.pallas_call(
        body,
        grid=(m // bm, n_out // bn),
        in_specs=[pl.BlockSpec((bm, k), lambda i, j: (i, 0)),
                  _w_spec(w, layer, k, bn, col_off // bn),
                  pl.BlockSpec((bm, head_dim), lambda i, j: (i % pos_tiles, 0)),
                  pl.BlockSpec((bm, head_dim), lambda i, j: (i % pos_tiles, 0))],
        out_specs=pl.BlockSpec((bm, bn), lambda i, j: (i, j)),
        out_shape=jax.ShapeDtypeStruct((m, n_out), BF16),
        compiler_params=_compiler_params(("parallel", "arbitrary"), blocks, 3 * _nbytes((bm, bn), F32)),
        name="matmul_rope",
    )(a, w, cos_full, sin_signed)


def _mm_swiglu_body(a_ref, wa_ref, wb_ref, o_ref):
    a = a_ref[...]
    ga = jnp.dot(a, wa_ref[...], preferred_element_type=F32)
    gb = jnp.dot(a, wb_ref[...], preferred_element_type=F32)
    o_ref[...] = (jax.nn.silu(ga) * gb).astype(o_ref.dtype)


def matmul_swiglu(a, w13, *, layer, bm=MM_BM, bn=MM_BN):
    m, k = a.shape
    f = w13.shape[-1] // 2
    bm = min(bm, m)
    assert m % bm == 0 and f % bn == 0
    blocks = _nbytes((bm, k), a.dtype) + 2 * _nbytes((k, bn), w13.dtype) + _nbytes((bm, bn), BF16)
    return pl.pallas_call(
        _mm_swiglu_body,
        grid=(m // bm, f // bn),
        in_specs=[pl.BlockSpec((bm, k), lambda i, j: (i, 0)),
                  _w_spec(w13, layer, k, bn, 0),
                  _w_spec(w13, layer, k, bn, f // bn)],
        out_specs=pl.BlockSpec((bm, bn), lambda i, j: (i, j)),
        out_shape=jax.ShapeDtypeStruct((m, f), BF16),
        compiler_params=_compiler_params(("parallel", "arbitrary"), blocks, 4 * _nbytes((bm, bn), F32)),
        name="matmul_swiglu",
    )(a, w13, w13)


def _mm_moe_swiglu_body(a_ref, wa_ref, wb_ref, gates_ref, o_ref, *, blocks_per_expert):
    a = a_ref[...]
    ga = jnp.dot(a, wa_ref[...], preferred_element_type=F32)
    gb = jnp.dot(a, wb_ref[...], preferred_element_type=F32)
    expert = pl.program_id(1) // blocks_per_expert
    gates = gates_ref[...]
    lane = lax.broadcasted_iota(jnp.int32, gates.shape, 1)
    gate = jnp.sum(jnp.where(lane == expert, gates, 0.0), axis=1, keepdims=True)
    o_ref[...] = (gate * (jax.nn.silu(ga) * gb)).astype(o_ref.dtype)


def matmul_moe_swiglu(a, w13, gates, *, layer, bm=MM_BM, bn=MM_BN):
    m, k = a.shape
    n_exp, f = w13.shape[1], w13.shape[-1] // 2
    bm = min(bm, m)
    assert m % bm == 0 and f % bn == 0
    bpe = f // bn
    blocks = (_nbytes((bm, k), a.dtype) + 2 * _nbytes((k, bn), w13.dtype) + _nbytes((bm, bn), BF16)
              + _nbytes((bm, V7X_LANES), F32))
    return pl.pallas_call(
        functools.partial(_mm_moe_swiglu_body, blocks_per_expert=bpe),
        grid=(m // bm, n_exp * bpe),
        in_specs=[pl.BlockSpec((bm, k), lambda i, j: (i, 0)),
                  pl.BlockSpec((None, None, k, bn), lambda i, j: (layer, j // bpe, 0, j % bpe)),
                  pl.BlockSpec((None, None, k, bn), lambda i, j: (layer, j // bpe, 0, j % bpe + bpe)),
                  pl.BlockSpec((bm, V7X_LANES), lambda i, j: (i, 0))],
        out_specs=pl.BlockSpec((bm, bn), lambda i, j: (i, j)),
        out_shape=jax.ShapeDtypeStruct((m, n_exp * f), BF16),
        compiler_params=_compiler_params(("parallel", "arbitrary"), blocks, 4 * _nbytes((bm, bn), F32)),
        name="matmul_moe_swiglu",
    )(a, w13, w13, gates)


def _mm_ple_body(a_ref, wg_ref, p_ref, wp_ref, x_ref, o_ref):
    gate = jnp.dot(a_ref[...], wg_ref[...], preferred_element_type=F32)
    proj = jnp.dot(p_ref[...].astype(BF16), wp_ref[...], preferred_element_type=F32)
    o_ref[...] = DEEPNORM_ALPHA * x_ref[...] + jax.nn.sigmoid(gate) * proj


def matmul_ple_residual(a, w_gate, p, w_proj, x, *, layer, bm=MM_BM, bn=MM_BN):
    m, k = a.shape
    n = w_gate.shape[-1]
    kp = p.shape[-1]
    bm = min(bm, m)
    assert m % bm == 0 and n % bn == 0 and x.shape == (m, n)
    blocks = (_nbytes((bm, k), a.dtype) + _nbytes((k, bn), w_gate.dtype) + _nbytes((bm, kp), p.dtype)
              + _nbytes((kp, bn), w_proj.dtype) + 2 * _nbytes((bm, bn), F32))
    return pl.pallas_call(
        _mm_ple_body,
        grid=(m // bm, n // bn),
        in_specs=[pl.BlockSpec((bm, k), lambda i, j: (i, 0)),
                  _w_spec(w_gate, layer, k, bn),
                  pl.BlockSpec((None, bm, kp), lambda i, j: (layer, i, 0)),
                  _w_spec(w_proj, layer, kp, bn),
                  pl.BlockSpec((bm, bn), lambda i, j: (i, j))],
        out_specs=pl.BlockSpec((bm, bn), lambda i, j: (i, j)),
        out_shape=jax.ShapeDtypeStruct((m, n), F32),
        compiler_params=_compiler_params(("parallel", "arbitrary"), blocks, 4 * _nbytes((bm, bn), F32)),
        name="matmul_ple_residual",
    )(a, w_gate, p, w_proj, x)


def _ln_body(y_ref, g_ref, b_ref, o_ref, ob_ref):
    y = y_ref[...]
    mu = jnp.mean(y, axis=-1, keepdims=True)
    d = y - mu
    var = jnp.mean(d * d, axis=-1, keepdims=True)
    out = d * lax.rsqrt(var + LN_EPS) * g_ref[...] + b_ref[...]
    o_ref[...] = out
    ob_ref[...] = out.astype(BF16)


def layer_norm(y, ln_g, ln_b, *, row, rows=LN_ROWS):
    m, d = y.shape
    rows = min(rows, m)
    assert m % rows == 0
    blocks = 2 * _nbytes((rows, d), F32) + _nbytes((rows, d), BF16) + 2 * _nbytes((1, d), F32)
    tile = pl.BlockSpec((rows, d), lambda i: (i, 0))
    vec = pl.BlockSpec((None, 1, d), lambda i: (row, 0, 0))
    return pl.pallas_call(
        _ln_body,
        grid=(m // rows,),
        in_specs=[tile, vec, vec],
        out_specs=[tile, tile],
        out_shape=[jax.ShapeDtypeStruct((m, d), F32), jax.ShapeDtypeStruct((m, d), BF16)],
        compiler_params=_compiler_params(("parallel",), blocks, 3 * _nbytes((rows, d), F32)),
        name="layer_norm",
    )(y, ln_g, ln_b)


def _retention_body(q_ref, k_ref, v_ref, g_ref, dmask_ref, qdec_ref, kdec_ref, cdec_ref, o_ref, state_ref,
                    *, n_chunks, chunk):
    @pl.when(pl.program_id(2) == 0)
    def _():
        state_ref[...] = jnp.zeros_like(state_ref)

    dmask = dmask_ref[...]
    qdec = qdec_ref[...]
    kdec = kdec_ref[...]
    cdec = cdec_ref[...]
    for c in range(n_chunks):
        rows = pl.ds(c * chunk, chunk)
        q = q_ref[rows, :]
        k = k_ref[rows, :]
        v = v_ref[rows, :]
        state = state_ref[...]
        scores = lax.dot_general(q, k, (((1,), (1,)), ((), ())), preferred_element_type=F32) * dmask
        inner = jnp.dot(scores.astype(BF16), v, preferred_element_type=F32)
        cross = jnp.dot((q.astype(F32) * qdec).astype(BF16), state.astype(BF16), preferred_element_type=F32)
        k_scaled = (k.astype(F32) * kdec).astype(BF16)
        update = lax.dot_general(k_scaled, v, (((0,), (0,)), ((), ())), preferred_element_type=F32)
        state_ref[...] = state * cdec + update
        out = inner + cross
        mu = jnp.mean(out, axis=-1, keepdims=True)
        d = out - mu
        var = jnp.mean(d * d, axis=-1, keepdims=True)
        normed = d * lax.rsqrt(var + LN_EPS)
        gate = g_ref[rows, :].astype(F32)
        o_ref[rows, :] = (jax.nn.silu(gate) * normed).astype(o_ref.dtype)


def retention_tables():
    h, c = RET_HEADS, RET_CHUNK
    log_gamma = jnp.log1p(-jnp.exp2(-5.0 - jnp.arange(h, dtype=F32)))
    pos = jnp.arange(c, dtype=F32)
    rel = pos[:, None] - pos[None, :]
    dmask = jnp.where(rel >= 0, jnp.exp(log_gamma[:, None, None] * jnp.maximum(rel, 0.0)), 0.0)
    qdec = jnp.exp(log_gamma[:, None] * (pos + 1.0))[:, :, None]
    kdec = jnp.exp(log_gamma[:, None] * (c - 1.0 - pos))[:, :, None]
    cdec = jnp.exp(log_gamma * c)[:, None, None]
    return (dmask,
            jnp.broadcast_to(qdec, (h, c, RET_QK_DIM)),
            jnp.broadcast_to(kdec, (h, c, RET_QK_DIM)),
            jnp.broadcast_to(cdec, (h, 1, RET_V_DIM)))


def retention(qk, vg, tables, *, batch, seq, rows=RET_ROWS):
    h, dk, dv, c = RET_HEADS, RET_QK_DIM, RET_V_DIM, RET_CHUNK
    rows = min(rows, seq)
    assert seq % rows == 0 and rows % c == 0
    steps = seq // rows
    dmask, qdec, kdec, cdec = tables
    row_tile = lambda b, hh, l: b * steps + l
    blocks = (2 * _nbytes((rows, dk), BF16) + 3 * _nbytes((rows, dv), BF16) + _nbytes((c, c), F32)
              + 2 * _nbytes((c, dk), F32) + _nbytes((8, dv), F32))
    return pl.pallas_call(
        functools.partial(_retention_body, n_chunks=rows // c, chunk=c),
        grid=(batch, h, steps),
        in_specs=[pl.BlockSpec((rows, dk), lambda b, hh, l: (row_tile(b, hh, l), hh)),
                  pl.BlockSpec((rows, dk), lambda b, hh, l: (row_tile(b, hh, l), h + hh)),
                  pl.BlockSpec((rows, dv), lambda b, hh, l: (row_tile(b, hh, l), hh)),
                  pl.BlockSpec((rows, dv), lambda b, hh, l: (row_tile(b, hh, l), h + hh)),
                  pl.BlockSpec((None, c, c), lambda b, hh, l: (hh, 0, 0)),
                  pl.BlockSpec((None, c, dk), lambda b, hh, l: (hh, 0, 0)),
                  pl.BlockSpec((None, c, dk), lambda b, hh, l: (hh, 0, 0)),
                  pl.BlockSpec((None, 1, dv), lambda b, hh, l: (hh, 0, 0))],
        out_specs=pl.BlockSpec((rows, dv), lambda b, hh, l: (row_tile(b, hh, l), hh)),
        out_shape=jax.ShapeDtypeStruct((batch * seq, h * dv), BF16),
        scratch_shapes=[pltpu.VMEM((dk, dv), F32)],
        compiler_params=_compiler_params(("parallel", "parallel", "arbitrary"), blocks,
                                         _nbytes((dk, dv), F32) * 4 + _nbytes((c, dv), F32) * 8),
        name="retention",
    )(qk, qk, vg, vg, dmask, qdec, kdec, cdec)


def _diff_attention_body(q_ref, k_ref, v_ref, lam_ref, g_ref, o_ref, max_ref, sum_ref, acc_ref, s_ref, p_ref,
                         *, tile, strip, lambda_init):
    qi = pl.program_id(2)
    d = DIFF_HEAD_DIM
    lanes = V7X_LANES
    contract_last = (((1,), (1,)), ((), ()))

    def scores(kt):
        rows_k = pl.ds(pl.multiple_of(kt * tile, tile), tile)
        for c in range(2):
            s_ref[c] = lax.dot_general(q_ref[:, c * d:(c + 1) * d], k_ref[rows_k, c * d:(c + 1) * d],
                                       contract_last, preferred_element_type=F32)
        return rows_k

    def block(c, r, cb, diagonal):
        if diagonal and cb * lanes >= (r + 1) * strip:
            return None
        s = s_ref[c, r * strip:(r + 1) * strip, cb * lanes:(cb + 1) * lanes]
        if diagonal and (cb + 1) * lanes - 1 > r * strip:
            row_id = r * strip + lax.broadcasted_iota(jnp.int32, (strip, lanes), 0)
            col_id = cb * lanes + lax.broadcasted_iota(jnp.int32, (strip, lanes), 1)
            s = jnp.where(row_id >= col_id, s, -jnp.inf)
        return s

    def max_sweep(kt, diagonal):
        scores(kt)
        for c in range(2):
            for r in range(tile // strip):
                rows = slice(r * strip, (r + 1) * strip)
                blocks = [block(c, r, cb, diagonal) for cb in range(tile // lanes)]
                max_ref[c, rows] = functools.reduce(jnp.maximum,
                                                    [max_ref[c, rows]] + [b for b in blocks if b is not None])

    def sum_sweep(kt, diagonal):
        rows_k = scores(kt)
        for c in range(2):
            for r in range(tile // strip):
                rows = slice(r * strip, (r + 1) * strip)
                row_max = max_ref[c, rows]
                total = sum_ref[c, rows]
                for cb in range(tile // lanes):
                    s = block(c, r, cb, diagonal)
                    if s is None:
                        p_ref[c, rows, cb * lanes:(cb + 1) * lanes] = jnp.zeros((strip, lanes), BF16)
                        continue
                    p = jnp.exp2(s - row_max)
                    total = total + p
                    p_ref[c, rows, cb * lanes:(cb + 1) * lanes] = p.astype(BF16)
                sum_ref[c, rows] = total
        v = v_ref[rows_k, :]
        for c in range(2):
            acc_ref[c] += jnp.dot(p_ref[c], v, preferred_element_type=F32)

    def loop(sweep):
        def body(kt, carry):
            sweep(kt, False)
            return carry
        lax.fori_loop(0, qi, body, 0)
        sweep(qi, True)

    max_ref[...] = jnp.full_like(max_ref, -jnp.inf)
    sum_ref[...] = jnp.zeros_like(sum_ref)
    acc_ref[...] = jnp.zeros_like(acc_ref)
    loop(max_sweep)
    for c in range(2):
        max_ref[c] = jnp.broadcast_to(jnp.max(max_ref[c], axis=1, keepdims=True), (tile, lanes))
    loop(sum_sweep)

    lam = lam_ref[...]
    lam_full = (jnp.exp(jnp.sum(lam[0:1] * lam[1:2], axis=1, keepdims=True))
                - jnp.exp(jnp.sum(lam[2:3] * lam[3:4], axis=1, keepdims=True)) + lambda_init)
    inv_l = [1.0 / jnp.sum(sum_ref[c], axis=1, keepdims=True) for c in range(2)]
    out = acc_ref[0] * inv_l[0] - lam_full * (acc_ref[1] * inv_l[1])
    y = out * lax.rsqrt(jnp.mean(out * out, axis=-1, keepdims=True) + LN_EPS)
    o_ref[...] = ((y * g_ref[...]) * (1.0 - lambda_init)).astype(o_ref.dtype)


def diff_attention(q, k, v, lam, subln_g, *, layer, batch, seq, lambda_init, tile=ATT_TILE, strip=ATT_STRIP):
    h, dv = DIFF_HEADS, DIFF_V_DIM
    tile = min(tile, seq)
    assert seq % tile == 0 and tile % strip == 0
    nq = seq // tile
    blocks = (2 * _nbytes((tile, dv), BF16) + 2 * _nbytes((seq, dv), BF16) + _nbytes((8, V7X_LANES), F32)
              + _nbytes((8, dv), F32))
    scratch = [pltpu.VMEM((2, tile, V7X_LANES), F32), pltpu.VMEM((2, tile, V7X_LANES), F32),
               pltpu.VMEM((2, tile, dv), F32), pltpu.VMEM((2, tile, tile), F32), pltpu.VMEM((2, tile, tile), BF16)]
    scratch_bytes = (2 * 2 * _nbytes((tile, V7X_LANES), F32) + 2 * _nbytes((tile, dv), F32)
                     + 2 * _nbytes((tile, tile), F32) + 2 * _nbytes((tile, tile), BF16))
    return pl.pallas_call(
        functools.partial(_diff_attention_body, tile=tile, strip=strip, lambda_init=lambda_init),
        grid=(batch, h, nq),
        in_specs=[pl.BlockSpec((tile, dv), lambda b, hh, qi: (b * nq + qi, hh)),
                  pl.BlockSpec((seq, dv), lambda b, hh, qi: (b, hh)),
                  pl.BlockSpec((seq, dv), lambda b, hh, qi: (b, hh)),
                  pl.BlockSpec((None, 4, DIFF_HEAD_DIM), lambda b, hh, qi: (layer, 0, 0)),
                  pl.BlockSpec((None, 1, dv), lambda b, hh, qi: (layer, 0, 0))],
        out_specs=pl.BlockSpec((tile, dv), lambda b, hh, qi: (b * nq + qi, hh)),
        out_shape=jax.ShapeDtypeStruct((batch * seq, h * dv), BF16),
        scratch_shapes=scratch,
        compiler_params=_compiler_params(("parallel", "parallel", "arbitrary"), blocks,
                                         scratch_bytes + 2 * _nbytes((tile, dv), F32)),
        name="diff_attention",
    )(q, k, v, lam, subln_g)


def _router_body(a_ref, w_ref, o_ref):
    logits = jnp.dot(a_ref[...], w_ref[...], preferred_element_type=F32)
    lane = lax.broadcasted_iota(jnp.int32, logits.shape, 1).astype(F32)
    logits = jnp.where(lane < N_EXPERTS, logits, -jnp.inf)
    top1 = jnp.max(logits, axis=1, keepdims=True)
    idx1 = jnp.min(jnp.where(logits == top1, lane, float(V7X_LANES)), axis=1, keepdims=True)
    rest = jnp.where(lane == idx1, -jnp.inf, logits)
    top2 = jnp.max(rest, axis=1, keepdims=True)
    idx2 = jnp.min(jnp.where(rest == top2, lane, float(V7X_LANES)), axis=1, keepdims=True)
    e = jnp.exp(top2 - top1)
    w1 = 1.0 / (1.0 + e)
    w2 = e / (1.0 + e)
    o_ref[...] = jnp.where(lane == idx1, w1, 0.0) + jnp.where(lane == idx2, w2, 0.0)


def router_gates(a, w_router_padded, *, layer, bm=MM_BM):
    m, k = a.shape
    bm = min(bm, m)
    assert m % bm == 0
    blocks = _nbytes((bm, k), a.dtype) + _nbytes((k, V7X_LANES), BF16) + _nbytes((bm, V7X_LANES), F32)
    return pl.pallas_call(
        _router_body,
        grid=(m // bm,),
        in_specs=[pl.BlockSpec((bm, k), lambda i: (i, 0)),
                  pl.BlockSpec((None, k, V7X_LANES), lambda i: (layer, 0, 0))],
        out_specs=pl.BlockSpec((bm, V7X_LANES), lambda i: (i, 0)),
        out_shape=jax.ShapeDtypeStruct((m, V7X_LANES), F32),
        compiler_params=_compiler_params(("parallel",), blocks, 8 * _nbytes((bm, V7X_LANES), F32)),
        name="router_gates",
    )(a, w_router_padded)


def _rope_tables(seq, dim):
    inv_freq = 1.0 / (ROPE_THETA ** (jnp.arange(0, dim, 2, dtype=F32) / dim))
    ang = jnp.arange(seq, dtype=F32)[:, None] * inv_freq[None, :]
    cos, sin = jnp.cos(ang), jnp.sin(ang)
    return jnp.concatenate([cos, cos], axis=-1), jnp.concatenate([-sin, sin], axis=-1)


def kernel(x, p, ret_w_in, ret_w_out, kv_w, diff_w_q, diff_w_out, diff_lambda, diff_subln_g, ffn_w13, ffn_w2,
           moe_router, moe_w13, moe_w2, ple_w_gate, ple_w_proj, ln_g, ln_b):
    batch, seq, d = x.shape
    assert d == D_MODEL
    tokens = batch * seq
    cos_r, sin_r = _rope_tables(seq, RET_QK_DIM)
    cos_d, sin_d = _rope_tables(seq, DIFF_HEAD_DIM)
    ret_tabs = retention_tables()

    bf = lambda w: w.astype(BF16)
    ret_w_in, ret_w_out, kv_w, diff_w_q, diff_w_out = map(bf, (ret_w_in, ret_w_out, kv_w, diff_w_q, diff_w_out))
    ffn_w13, ffn_w2, moe_w13, ple_w_gate, ple_w_proj = map(bf, (ffn_w13, ffn_w2, moe_w13, ple_w_gate, ple_w_proj))
    moe_w2 = bf(moe_w2).reshape(moe_w2.shape[0], N_EXPERTS * D_EXPERT, D_MODEL)
    router_w = jnp.pad(bf(moe_router), ((0, 0), (0, 0), (0, V7X_LANES - N_EXPERTS)))
    ln_g = ln_g.reshape(DEPTH * 3, 1, D_MODEL)
    ln_b = ln_b.reshape(DEPTH * 3, 1, D_MODEL)
    subln_g = diff_subln_g.reshape(-1, 1, DIFF_V_DIM)
    p = p.reshape(DEPTH, tokens, PLE_DIM)

    x = x.reshape(tokens, D_MODEL)
    xb = x.astype(BF16)
    qk_w = RET_HEADS * RET_QK_DIM
    v_w = RET_HEADS * RET_V_DIM
    kd = 2 * DIFF_HEADS * DIFF_HEAD_DIM
    vd = DIFF_HEADS * DIFF_V_DIM
    k_sh = v_sh = None
    for i in range(DEPTH):
        if i < N_A_LAYERS:
            qk = matmul_rope(xb, ret_w_in, cos_r, sin_r, layer=i, col_off=0, n_out=2 * qk_w,
                             head_dim=RET_QK_DIM, split_col=qk_w, scale_lo=1.0, scale_hi=RET_QK_DIM ** -0.5)
            vg = matmul(xb, ret_w_in, layer=i, col_off=2 * qk_w, n_out=2 * v_w, out_dtype=BF16)
            gated = retention(qk, vg, ret_tabs, batch=batch, seq=seq)
            y = matmul_residual(gated, ret_w_out, x, layer=i)
        else:
            j = i - N_A_LAYERS
            lambda_init = 0.8 - 0.6 * math.exp(-0.3 * i)
            q_scale = DIFF_HEAD_DIM ** -0.5 * LOG2_E
            q = matmul_rope(xb, diff_w_q, cos_d, sin_d, layer=j, n_out=kd, head_dim=DIFF_HEAD_DIM,
                            scale_lo=q_scale, scale_hi=q_scale)
            att = diff_attention(q, k_sh, v_sh, diff_lambda, subln_g, layer=j, batch=batch, seq=seq,
                                 lambda_init=lambda_init)
            y = matmul_residual(att, diff_w_out, x, layer=j)
        x, xb = layer_norm(y, ln_g, ln_b, row=3 * i)
        if i % 2 == 0:
            hidden = matmul_swiglu(xb, ffn_w13, layer=i // 2)
            y = matmul_residual(hidden, ffn_w2, x, layer=i // 2)
        else:
            gates = router_gates(xb, router_w, layer=i // 2)
            hidden = matmul_moe_swiglu(xb, moe_w13, gates, layer=i // 2)
            y = matmul_residual(hidden, moe_w2, x, layer=i // 2)
        x, xb = layer_norm(y, ln_g, ln_b, row=3 * i + 1)
        y = matmul_ple_residual(xb, ple_w_gate, p, ple_w_proj, x, layer=i)
        x, xb = layer_norm(y, ln_g, ln_b, row=3 * i + 2)
        if i == N_A_LAYERS - 1:
            k_sh = matmul_rope(xb, kv_w, cos_d, sin_d, col_off=0, n_out=kd, head_dim=DIFF_HEAD_DIM)
            v_sh = matmul(xb, kv_w, col_off=kd, n_out=vd, out_dtype=BF16)
    return x.reshape(batch, seq, D_MODEL)
```

```python
import functools
import math

import jax
import jax.numpy as jnp
from jax import lax
from jax.experimental import pallas as pl
from jax.experimental.pallas import tpu as pltpu

F32 = jnp.float32
BF16 = jnp.bfloat16

D_MODEL = 4096
DEPTH = 4
N_A_LAYERS = DEPTH // 2
RET_HEADS = 16
RET_QK_DIM = D_MODEL // RET_HEADS
RET_V_DIM = 2 * D_MODEL // RET_HEADS
RET_CHUNK = 128
DIFF_HEADS = 16
DIFF_HEAD_DIM = D_MODEL // (2 * DIFF_HEADS)
DIFF_V_DIM = 2 * DIFF_HEAD_DIM
ROPE_THETA = 10000.0
D_FF = 2 * D_MODEL
N_EXPERTS = 8
TOP_K = 2
D_EXPERT = D_MODEL // 4
PLE_DIM = 256
LN_EPS = 1e-5
DEEPNORM_ALPHA = (2.0 * DEPTH) ** 0.25

V7X_LANES = 128
V7X_VMEM_BYTES = 64 * 1024 * 1024
V7X_VMEM_CAP = V7X_VMEM_BYTES - 6 * 1024 * 1024

MM_BM = 1024
MM_BM_WIDE_K = 512
MM_BN = 512
MM_BN_WIDE = 1024
LN_ROWS = 256
RET_ROWS = 512
ATT_TILE = 512
ATT_STRIP = 64
LOG2_E = math.log2(math.e)


def _nbytes(shape, dtype):
    return math.prod(shape) * jnp.dtype(dtype).itemsize


def _compiler_params(semantics, block_bytes, temp_bytes):
    need = 2 * block_bytes + temp_bytes + (2 << 20)
    return pltpu.CompilerParams(
        dimension_semantics=semantics,
        vmem_limit_bytes=int(min(max(need, 16 << 20), V7X_VMEM_CAP)),
    )


def _w_spec(w, layer, k, bn, col_block_off=0):
    if w.ndim == 3:
        return pl.BlockSpec((None, k, bn), lambda i, j: (layer, 0, j + col_block_off))
    return pl.BlockSpec((k, bn), lambda i, j: (0, j + col_block_off))


def _hosting_cast(body, n_in):
    def hosted(*refs):
        body(*refs[:n_in], refs[n_in + 1])
        refs[n_in + 2][...] = refs[n_in][...].astype(BF16)
    return hosted


def _launch_matmul(body, name, grid, in_specs, args, out_block, out_shape, block_bytes, temp_bytes, cast=None):
    out_spec = pl.BlockSpec(out_block, lambda i, j: (i, j))
    semantics = ("parallel", "arbitrary")
    if cast is None:
        return pl.pallas_call(
            body, grid=grid, in_specs=in_specs, out_specs=out_spec, out_shape=out_shape,
            compiler_params=_compiler_params(semantics, block_bytes, temp_bytes), name=name,
        )(*args)
    src, part, n_parts = cast
    rows, cols = src.shape[0] // n_parts, src.shape[1]
    steps = grid[0] * grid[1]
    slab = rows // steps
    assert src.shape[0] % n_parts == 0 and rows % steps == 0 and slab % 16 == 0
    block_bytes += _nbytes((slab, cols), F32) + _nbytes((slab, cols), BF16)
    return pl.pallas_call(
        _hosting_cast(body, len(in_specs)),
        grid=grid,
        in_specs=list(in_specs) + [pl.BlockSpec((slab, cols), lambda i, j: (part * steps + i * grid[1] + j, 0))],
        out_specs=[out_spec, pl.BlockSpec((slab, cols), lambda i, j: (i * grid[1] + j, 0))],
        out_shape=[out_shape, jax.ShapeDtypeStruct((rows, cols), BF16)],
        compiler_params=_compiler_params(semantics, block_bytes, temp_bytes), name=name + "_cast",
    )(*args, src)


def _mm_plain_body(a_ref, w_ref, o_ref):
    o_ref[...] = jnp.dot(a_ref[...], w_ref[...], preferred_element_type=F32).astype(o_ref.dtype)


def matmul(a, w, *, layer=None, col_off=0, n_out, out_dtype, cast=None):
    m, k = a.shape
    bm, bn = (MM_BM, MM_BN_WIDE) if k <= D_MODEL else (MM_BM_WIDE_K, MM_BN)
    bm = min(bm, m)
    assert m % bm == 0 and n_out % bn == 0 and col_off % bn == 0
    blocks = _nbytes((bm, k), a.dtype) + _nbytes((k, bn), w.dtype) + _nbytes((bm, bn), out_dtype)
    return _launch_matmul(
        _mm_plain_body, "matmul", (m // bm, n_out // bn),
        [pl.BlockSpec((bm, k), lambda i, j: (i, 0)), _w_spec(w, layer, k, bn, col_off // bn)], (a, w),
        (bm, bn), jax.ShapeDtypeStruct((m, n_out), out_dtype), blocks, 2 * _nbytes((bm, bn), F32), cast)


def _mm_residual_body(a_ref, w_ref, x_ref, o_ref):
    o_ref[...] = DEEPNORM_ALPHA * x_ref[...] + jnp.dot(a_ref[...], w_ref[...], preferred_element_type=F32)


def matmul_residual(a, w, x, *, layer, cast=None):
    m, k = a.shape
    n = w.shape[-1]
    bm, bn = (MM_BM, MM_BN) if k <= D_MODEL else (MM_BM_WIDE_K, MM_BN)
    bm = min(bm, m)
    assert m % bm == 0 and n % bn == 0 and x.shape == (m, n)
    blocks = _nbytes((bm, k), a.dtype) + _nbytes((k, bn), w.dtype) + 2 * _nbytes((bm, bn), F32)
    return _launch_matmul(
        _mm_residual_body, "matmul_residual", (m // bm, n // bn),
        [pl.BlockSpec((bm, k), lambda i, j: (i, 0)), _w_spec(w, layer, k, bn),
         pl.BlockSpec((bm, bn), lambda i, j: (i, j))], (a, w, x),
        (bm, bn), jax.ShapeDtypeStruct((m, n), F32), blocks, 2 * _nbytes((bm, bn), F32), cast)


def _mm_rope_body(a_ref, w_ref, cos_ref, sin_ref, o_ref, *, head_dim, split_block, scale_lo, scale_hi):
    acc = jnp.dot(a_ref[...], w_ref[...], preferred_element_type=F32)
    cos = cos_ref[...]
    sin = sin_ref[...]
    if scale_lo == scale_hi:
        scale = scale_lo
    else:
        scale = jnp.where(pl.program_id(1) < split_block, scale_lo, scale_hi)
    for h in range(acc.shape[1] // head_dim):
        cols = slice(h * head_dim, (h + 1) * head_dim)
        xb = acc[:, cols]
        rot = pltpu.roll(xb, head_dim // 2, axis=1)
        o_ref[:, cols] = ((xb * cos + rot * sin) * scale).astype(o_ref.dtype)


def matmul_rope(a, w, cos_full, sin_signed, *, layer=None, col_off=0, n_out, head_dim,
                split_col=0, scale_lo=1.0, scale_hi=1.0, bm=MM_BM, bn=MM_BN_WIDE):
    m, k = a.shape
    seq = cos_full.shape[0]
    bm = min(bm, seq)
    assert m % bm == 0 and seq % bm == 0 and n_out % bn == 0 and bn % head_dim == 0
    assert col_off % bn == 0 and split_col % bn == 0
    pos_tiles = seq // bm
    blocks = (_nbytes((bm, k), a.dtype) + _nbytes((k, bn), w.dtype) + _nbytes((bm, bn), BF16)
              + 2 * _nbytes((bm, head_dim), F32))
    body = functools.partial(_mm_rope_body, head_dim=head_dim, split_block=split_col // bn,
                             scale_lo=scale_lo, scale_hi=scale_hi)
    return _launch_matmul(
        body, "matmul_rope", (m // bm, n_out // bn),
        [pl.BlockSpec((bm, k), lambda i, j: (i, 0)),
         _w_spec(w, layer, k, bn, col_off // bn),
         pl.BlockSpec((bm, head_dim), lambda i, j: (i % pos_tiles, 0)),
         pl.BlockSpec((bm, head_dim), lambda i, j: (i % pos_tiles, 0))], (a, w, cos_full, sin_signed),
        (bm, bn), jax.ShapeDtypeStruct((m, n_out), BF16), blocks, 3 * _nbytes((bm, bn), F32))


def _mm_swiglu_body(a_ref, wa_ref, wb_ref, o_ref):
    a = a_ref[...]
    ga = jnp.dot(a, wa_ref[...], preferred_element_type=F32)
    gb = jnp.dot(a, wb_ref[...], preferred_element_type=F32)
    o_ref[...] = (jax.nn.silu(ga) * gb).astype(o_ref.dtype)


def matmul_swiglu(a, w13, *, layer=None, bm=MM_BM, bn=MM_BN, cast=None):
    m, k = a.shape
    f = w13.shape[-1] // 2
    bm = min(bm, m)
    assert m % bm == 0 and f % bn == 0
    blocks = _nbytes((bm, k), a.dtype) + 2 * _nbytes((k, bn), w13.dtype) + _nbytes((bm, bn), BF16)
    return _launch_matmul(
        _mm_swiglu_body, "matmul_swiglu", (m // bm, f // bn),
        [pl.BlockSpec((bm, k), lambda i, j: (i, 0)), _w_spec(w13, layer, k, bn, 0),
         _w_spec(w13, layer, k, bn, f // bn)], (a, w13, w13),
        (bm, bn), jax.ShapeDtypeStruct((m, f), BF16), blocks, 4 * _nbytes((bm, bn), F32), cast)


def _mm_moe_swiglu_body(a_ref, wa_ref, wb_ref, gates_ref, o_ref, *, blocks_per_expert):
    a = a_ref[...]
    ga = jnp.dot(a, wa_ref[...], preferred_element_type=F32)
    gb = jnp.dot(a, wb_ref[...], preferred_element_type=F32)
    expert = pl.program_id(1) // blocks_per_expert
    gates = gates_ref[...]
    lane = lax.broadcasted_iota(jnp.int32, gates.shape, 1)
    gate = jnp.sum(jnp.where(lane == expert, gates, 0.0), axis=1, keepdims=True)
    o_ref[...] = (gate * (jax.nn.silu(ga) * gb)).astype(o_ref.dtype)


def matmul_moe_swiglu(a, w13, gates, *, bm=MM_BM, bn=MM_BN, cast=None):
    m, k = a.shape
    n_exp, f = w13.shape[0], w13.shape[-1] // 2
    bm = min(bm, m)
    assert m % bm == 0 and f % bn == 0
    bpe = f // bn
    blocks = (_nbytes((bm, k), a.dtype) + 2 * _nbytes((k, bn), w13.dtype) + _nbytes((bm, bn), BF16)
              + _nbytes((bm, V7X_LANES), F32))
    return _launch_matmul(
        functools.partial(_mm_moe_swiglu_body, blocks_per_expert=bpe), "matmul_moe_swiglu",
        (m // bm, n_exp * bpe),
        [pl.BlockSpec((bm, k), lambda i, j: (i, 0)),
         pl.BlockSpec((None, k, bn), lambda i, j: (j // bpe, 0, j % bpe)),
         pl.BlockSpec((None, k, bn), lambda i, j: (j // bpe, 0, j % bpe + bpe)),
         pl.BlockSpec((bm, V7X_LANES), lambda i, j: (i, 0))], (a, w13, w13, gates),
        (bm, bn), jax.ShapeDtypeStruct((m, n_exp * f), BF16), blocks, 4 * _nbytes((bm, bn), F32), cast)


def _mm_ple_body(a_ref, wg_ref, p_ref, wp_ref, x_ref, o_ref):
    gate = jnp.dot(a_ref[...], wg_ref[...], preferred_element_type=F32)
    proj = jnp.dot(p_ref[...].astype(BF16), wp_ref[...], preferred_element_type=F32)
    o_ref[...] = DEEPNORM_ALPHA * x_ref[...] + jax.nn.sigmoid(gate) * proj


def matmul_ple_residual(a, w_gate, p, w_proj, x, *, layer, bm=MM_BM, bn=MM_BN, cast=None):
    m, k = a.shape
    n = w_gate.shape[-1]
    kp = p.shape[-1]
    bm = min(bm, m)
    assert m % bm == 0 and n % bn == 0 and x.shape == (m, n)
    blocks = (_nbytes((bm, k), a.dtype) + _nbytes((k, bn), w_gate.dtype) + _nbytes((bm, kp), p.dtype)
              + _nbytes((kp, bn), w_proj.dtype) + 2 * _nbytes((bm, bn), F32))
    return _launch_matmul(
        _mm_ple_body, "matmul_ple_residual", (m // bm, n // bn),
        [pl.BlockSpec((bm, k), lambda i, j: (i, 0)),
         _w_spec(w_gate, layer, k, bn),
         pl.BlockSpec((None, bm, kp), lambda i, j: (layer, i, 0)),
         _w_spec(w_proj, layer, kp, bn),
         pl.BlockSpec((bm, bn), lambda i, j: (i, j))], (a, w_gate, p, w_proj, x),
        (bm, bn), jax.ShapeDtypeStruct((m, n), F32), blocks, 4 * _nbytes((bm, bn), F32), cast)


def _ln_body(y_ref, g_ref, b_ref, o_ref, ob_ref):
    y = y_ref[...]
    mu = jnp.mean(y, axis=-1, keepdims=True)
    d = y - mu
    var = jnp.mean(d * d, axis=-1, keepdims=True)
    out = d * lax.rsqrt(var + LN_EPS) * g_ref[...] + b_ref[...]
    o_ref[...] = out
    ob_ref[...] = out.astype(BF16)


def layer_norm(y, ln_g, ln_b, *, row, rows=LN_ROWS):
    m, d = y.shape
    rows = min(rows, m)
    assert m % rows == 0
    blocks = 2 * _nbytes((rows, d), F32) + _nbytes((rows, d), BF16) + 2 * _nbytes((1, d), F32)
    tile = pl.BlockSpec((rows, d), lambda i: (i, 0))
    vec = pl.BlockSpec((None, 1, d), lambda i: (row, 0, 0))
    return pl.pallas_call(
        _ln_body,
        grid=(m // rows,),
        in_specs=[tile, vec, vec],
        out_specs=[tile, tile],
        out_shape=[jax.ShapeDtypeStruct((m, d), F32), jax.ShapeDtypeStruct((m, d), BF16)],
        compiler_params=_compiler_params(("parallel",), blocks, 3 * _nbytes((rows, d), F32)),
        name="layer_norm",
    )(y, ln_g, ln_b)


def _retention_body(q_ref, k_ref, v_ref, g_ref, dmask_ref, qdec_ref, kdec_ref, cdec_ref, o_ref, state_ref,
                    *, n_chunks, chunk):
    @pl.when(pl.program_id(2) == 0)
    def _():
        state_ref[...] = jnp.zeros_like(state_ref)

    dmask = dmask_ref[...]
    qdec = qdec_ref[...]
    kdec = kdec_ref[...]
    cdec = cdec_ref[...]
    for c in range(n_chunks):
        rows = pl.ds(c * chunk, chunk)
        q = q_ref[rows, :]
        k = k_ref[rows, :]
        v = v_ref[rows, :]
        state = state_ref[...]
        scores = lax.dot_general(q, k, (((1,), (1,)), ((), ())), preferred_element_type=F32) * dmask
        inner = jnp.dot(scores.astype(BF16), v, preferred_element_type=F32)
        cross = jnp.dot((q.astype(F32) * qdec).astype(BF16), state.astype(BF16), preferred_element_type=F32)
        k_scaled = (k.astype(F32) * kdec).astype(BF16)
        update = lax.dot_general(k_scaled, v, (((0,), (0,)), ((), ())), preferred_element_type=F32)
        state_ref[...] = state * cdec + update
        out = inner + cross
        mu = jnp.mean(out, axis=-1, keepdims=True)
        d = out - mu
        var = jnp.mean(d * d, axis=-1, keepdims=True)
        normed = d * lax.rsqrt(var + LN_EPS)
        gate = g_ref[rows, :].astype(F32)
        o_ref[rows, :] = (jax.nn.silu(gate) * normed).astype(o_ref.dtype)


def retention_tables():
    h, c = RET_HEADS, RET_CHUNK
    log_gamma = jnp.log1p(-jnp.exp2(-5.0 - jnp.arange(h, dtype=F32)))
    pos = jnp.arange(c, dtype=F32)
    rel = pos[:, None] - pos[None, :]
    dmask = jnp.where(rel >= 0, jnp.exp(log_gamma[:, None, None] * jnp.maximum(rel, 0.0)), 0.0)
    qdec = jnp.exp(log_gamma[:, None] * (pos + 1.0))[:, :, None]
    kdec = jnp.exp(log_gamma[:, None] * (c - 1.0 - pos))[:, :, None]
    cdec = jnp.exp(log_gamma * c)[:, None, None]
    return (dmask,
            jnp.broadcast_to(qdec, (h, c, RET_QK_DIM)),
            jnp.broadcast_to(kdec, (h, c, RET_QK_DIM)),
            jnp.broadcast_to(cdec, (h, 1, RET_V_DIM)))


def retention(qk, vg, tables, *, batch, seq, rows=RET_ROWS):
    h, dk, dv, c = RET_HEADS, RET_QK_DIM, RET_V_DIM, RET_CHUNK
    rows = min(rows, seq)
    assert seq % rows == 0 and rows % c == 0
    steps = seq // rows
    dmask, qdec, kdec, cdec = tables
    row_tile = lambda b, hh, l: b * steps + l
    blocks = (2 * _nbytes((rows, dk), BF16) + 3 * _nbytes((rows, dv), BF16) + _nbytes((c, c), F32)
              + 2 * _nbytes((c, dk), F32) + _nbytes((8, dv), F32))
    return pl.pallas_call(
        functools.partial(_retention_body, n_chunks=rows // c, chunk=c),
        grid=(batch, h, steps),
        in_specs=[pl.BlockSpec((rows, dk), lambda b, hh, l: (row_tile(b, hh, l), hh)),
                  pl.BlockSpec((rows, dk), lambda b, hh, l: (row_tile(b, hh, l), h + hh)),
                  pl.BlockSpec((rows, dv), lambda b, hh, l: (row_tile(b, hh, l), hh)),
                  pl.BlockSpec((rows, dv), lambda b, hh, l: (row_tile(b, hh, l), h + hh)),
                  pl.BlockSpec((None, c, c), lambda b, hh, l: (hh, 0, 0)),
                  pl.BlockSpec((None, c, dk), lambda b, hh, l: (hh, 0, 0)),
                  pl.BlockSpec((None, c, dk), lambda b, hh, l: (hh, 0, 0)),
                  pl.BlockSpec((None, 1, dv), lambda b, hh, l: (hh, 0, 0))],
        out_specs=pl.BlockSpec((rows, dv), lambda b, hh, l: (row_tile(b, hh, l), hh)),
        out_shape=jax.ShapeDtypeStruct((batch * seq, h * dv), BF16),
        scratch_shapes=[pltpu.VMEM((dk, dv), F32)],
        compiler_params=_compiler_params(("parallel", "parallel", "arbitrary"), blocks,
                                         _nbytes((dk, dv), F32) * 4 + _nbytes((c, dv), F32) * 8),
        name="retention",
    )(qk, qk, vg, vg, dmask, qdec, kdec, cdec)


def _diff_attention_body(q_ref, k_ref, v_ref, lam_ref, g_ref, o_ref, max_ref, sum_ref, acc_ref, s_ref, p_ref,
                         *, tile, strip, lambda_init):
    qi = pl.program_id(2)
    d = DIFF_HEAD_DIM
    lanes = V7X_LANES
    contract_last = (((1,), (1,)), ((), ()))

    def scores(kt):
        rows_k = pl.ds(pl.multiple_of(kt * tile, tile), tile)
        for c in range(2):
            s_ref[c] = lax.dot_general(q_ref[:, c * d:(c + 1) * d], k_ref[rows_k, c * d:(c + 1) * d],
                                       contract_last, preferred_element_type=F32)
        return rows_k

    def block(c, r, cb, diagonal):
        if diagonal and cb * lanes >= (r + 1) * strip:
            return None
        s = s_ref[c, r * strip:(r + 1) * strip, cb * lanes:(cb + 1) * lanes]
        if diagonal and (cb + 1) * lanes - 1 > r * strip:
            row_id = r * strip + lax.broadcasted_iota(jnp.int32, (strip, lanes), 0)
            col_id = cb * lanes + lax.broadcasted_iota(jnp.int32, (strip, lanes), 1)
            s = jnp.where(row_id >= col_id, s, -jnp.inf)
        return s

    def max_sweep(kt, diagonal):
        scores(kt)
        for c in range(2):
            for r in range(tile // strip):
                rows = slice(r * strip, (r + 1) * strip)
                blocks = [block(c, r, cb, diagonal) for cb in range(tile // lanes)]
                max_ref[c, rows] = functools.reduce(jnp.maximum,
                                                    [max_ref[c, rows]] + [b for b in blocks if b is not None])

    def sum_sweep(kt, diagonal):
        rows_k = scores(kt)
        for c in range(2):
            for r in range(tile // strip):
                rows = slice(r * strip, (r + 1) * strip)
                row_max = max_ref[c, rows]
                total = sum_ref[c, rows]
                for cb in range(tile // lanes):
                    s = block(c, r, cb, diagonal)
                    if s is None:
                        p_ref[c, rows, cb * lanes:(cb + 1) * lanes] = jnp.zeros((strip, lanes), BF16)
                        continue
                    p = jnp.exp2(s - row_max)
                    total = total + p
                    p_ref[c, rows, cb * lanes:(cb + 1) * lanes] = p.astype(BF16)
                sum_ref[c, rows] = total
        v = v_ref[rows_k, :]
        for c in range(2):
            acc_ref[c] += jnp.dot(p_ref[c], v, preferred_element_type=F32)

    def loop(sweep):
        def body(kt, carry):
            sweep(kt, False)
            return carry
        lax.fori_loop(0, qi, body, 0)
        sweep(qi, True)

    max_ref[...] = jnp.full_like(max_ref, -jnp.inf)
    sum_ref[...] = jnp.zeros_like(sum_ref)
    acc_ref[...] = jnp.zeros_like(acc_ref)
    loop(max_sweep)
    for c in range(2):
        max_ref[c] = jnp.broadcast_to(jnp.max(max_ref[c], axis=1, keepdims=True), (tile, lanes))
    loop(sum_sweep)

    lam = lam_ref[...]
    lam_full = (jnp.exp(jnp.sum(lam[0:1] * lam[1:2], axis=1, keepdims=True))
                - jnp.exp(jnp.sum(lam[2:3] * lam[3:4], axis=1, keepdims=True)) + lambda_init)
    inv_l = [1.0 / jnp.sum(sum_ref[c], axis=1, keepdims=True) for c in range(2)]
    out = acc_ref[0] * inv_l[0] - lam_full * (acc_ref[1] * inv_l[1])
    y = out * lax.rsqrt(jnp.mean(out * out, axis=-1, keepdims=True) + LN_EPS)
    o_ref[...] = ((y * g_ref[...]) * (1.0 - lambda_init)).astype(o_ref.dtype)


def diff_attention(q, k, v, lam, subln_g, *, layer, batch, seq, lambda_init, tile=ATT_TILE, strip=ATT_STRIP):
    h, dv = DIFF_HEADS, DIFF_V_DIM
    tile = min(tile, seq)
    assert seq % tile == 0 and tile % strip == 0
    nq = seq // tile
    blocks = (2 * _nbytes((tile, dv), BF16) + 2 * _nbytes((seq, dv), BF16) + _nbytes((8, V7X_LANES), F32)
              + _nbytes((8, dv), F32))
    scratch = [pltpu.VMEM((2, tile, V7X_LANES), F32), pltpu.VMEM((2, tile, V7X_LANES), F32),
               pltpu.VMEM((2, tile, dv), F32), pltpu.VMEM((2, tile, tile), F32), pltpu.VMEM((2, tile, tile), BF16)]
    scratch_bytes = (2 * 2 * _nbytes((tile, V7X_LANES), F32) + 2 * _nbytes((tile, dv), F32)
                     + 2 * _nbytes((tile, tile), F32) + 2 * _nbytes((tile, tile), BF16))
    return pl.pallas_call(
        functools.partial(_diff_attention_body, tile=tile, strip=strip, lambda_init=lambda_init),
        grid=(batch, h, nq),
        in_specs=[pl.BlockSpec((tile, dv), lambda b, hh, qi: (b * nq + qi, hh)),
                  pl.BlockSpec((seq, dv), lambda b, hh, qi: (b, hh)),
                  pl.BlockSpec((seq, dv), lambda b, hh, qi: (b, hh)),
                  pl.BlockSpec((None, 4, DIFF_HEAD_DIM), lambda b, hh, qi: (layer, 0, 0)),
                  pl.BlockSpec((None, 1, dv), lambda b, hh, qi: (layer, 0, 0))],
        out_specs=pl.BlockSpec((tile, dv), lambda b, hh, qi: (b * nq + qi, hh)),
        out_shape=jax.ShapeDtypeStruct((batch * seq, h * dv), BF16),
        scratch_shapes=scratch,
        compiler_params=_compiler_params(("parallel", "parallel", "arbitrary"), blocks,
                                         scratch_bytes + 2 * _nbytes((tile, dv), F32)),
        name="diff_attention",
    )(q, k, v, lam, subln_g)


def _router_body(a_ref, w_ref, o_ref):
    logits = jnp.dot(a_ref[...], w_ref[...], preferred_element_type=F32)
    lane = lax.broadcasted_iota(jnp.int32, logits.shape, 1).astype(F32)
    logits = jnp.where(lane < N_EXPERTS, logits, -jnp.inf)
    top1 = jnp.max(logits, axis=1, keepdims=True)
    idx1 = jnp.min(jnp.where(logits == top1, lane, float(V7X_LANES)), axis=1, keepdims=True)
    rest = jnp.where(lane == idx1, -jnp.inf, logits)
    top2 = jnp.max(rest, axis=1, keepdims=True)
    idx2 = jnp.min(jnp.where(rest == top2, lane, float(V7X_LANES)), axis=1, keepdims=True)
    e = jnp.exp(top2 - top1)
    w1 = 1.0 / (1.0 + e)
    w2 = e / (1.0 + e)
    o_ref[...] = jnp.where(lane == idx1, w1, 0.0) + jnp.where(lane == idx2, w2, 0.0)


def router_gates(a, w_router_padded, *, layer, bm=MM_BM):
    m, k = a.shape
    bm = min(bm, m)
    assert m % bm == 0
    blocks = _nbytes((bm, k), a.dtype) + _nbytes((k, V7X_LANES), BF16) + _nbytes((bm, V7X_LANES), F32)
    return pl.pallas_call(
        _router_body,
        grid=(m // bm,),
        in_specs=[pl.BlockSpec((bm, k), lambda i: (i, 0)),
                  pl.BlockSpec((None, k, V7X_LANES), lambda i: (layer, 0, 0))],
        out_specs=pl.BlockSpec((bm, V7X_LANES), lambda i: (i, 0)),
        out_shape=jax.ShapeDtypeStruct((m, V7X_LANES), F32),
        compiler_params=_compiler_params(("parallel",), blocks, 8 * _nbytes((bm, V7X_LANES), F32)),
        name="router_gates",
    )(a, w_router_padded)


def _rope_tables(seq, dim):
    inv_freq = 1.0 / (ROPE_THETA ** (jnp.arange(0, dim, 2, dtype=F32) / dim))
    ang = jnp.arange(seq, dtype=F32)[:, None] * inv_freq[None, :]
    cos, sin = jnp.cos(ang), jnp.sin(ang)
    return jnp.concatenate([cos, cos], axis=-1), jnp.concatenate([-sin, sin], axis=-1)


def kernel(x, p, ret_w_in, ret_w_out, kv_w, diff_w_q, diff_w_out, diff_lambda, diff_subln_g, ffn_w13, ffn_w2,
           moe_router, moe_w13, moe_w2, ple_w_gate, ple_w_proj, ln_g, ln_b):
    batch, seq, d = x.shape
    assert d == D_MODEL
    tokens = batch * seq
    cos_r, sin_r = _rope_tables(seq, RET_QK_DIM)
    cos_d, sin_d = _rope_tables(seq, DIFF_HEAD_DIM)
    ret_tabs = retention_tables()

    w_in0 = ret_w_in[0].astype(BF16)
    ple_w_proj = ple_w_proj.astype(BF16)
    router_w = jnp.pad(moe_router.astype(BF16), ((0, 0), (0, 0), (0, V7X_LANES - N_EXPERTS)))
    flat = lambda w: w.reshape(-1, w.shape[-1])
    n_ret, n_dense, n_moe = ret_w_out.shape[0], ffn_w13.shape[0], moe_w13.shape[0]
    ln_g = ln_g.reshape(DEPTH * 3, 1, D_MODEL)
    ln_b = ln_b.reshape(DEPTH * 3, 1, D_MODEL)
    subln_g = diff_subln_g.reshape(-1, 1, DIFF_V_DIM)
    p = p.reshape(DEPTH, tokens, PLE_DIM)

    x = x.reshape(tokens, D_MODEL)
    xb = x.astype(BF16)
    qk_w = RET_HEADS * RET_QK_DIM
    v_w = RET_HEADS * RET_V_DIM
    kd = 2 * DIFF_HEADS * DIFF_HEAD_DIM
    vd = DIFF_HEADS * DIFF_V_DIM
    q_scale = DIFF_HEAD_DIM ** -0.5 * LOG2_E

    def retention_layer(i, w_in, w_out, x, xb, vg_cast, out_cast):
        qk = matmul_rope(xb, w_in, cos_r, sin_r, col_off=0, n_out=2 * qk_w, head_dim=RET_QK_DIM,
                         split_col=qk_w, scale_lo=1.0, scale_hi=RET_QK_DIM ** -0.5)
        vg, cast_a = matmul(xb, w_in, col_off=2 * qk_w, n_out=2 * v_w, out_dtype=BF16, cast=vg_cast)
        w_out = w_out if w_out is not None else cast_a.reshape(ret_w_out.shape)
        gated = retention(qk, vg, ret_tabs, batch=batch, seq=seq)
        y, cast_b = matmul_residual(gated, w_out, x, layer=i, cast=out_cast)
        return y, w_out, cast_a, cast_b

    def attention_layer(j, w_q, w_out, k_sh, v_sh, x, xb, out_cast):
        lambda_init = 0.8 - 0.6 * math.exp(-0.3 * (j + N_A_LAYERS))
        q = matmul_rope(xb, w_q, cos_d, sin_d, layer=j, n_out=kd, head_dim=DIFF_HEAD_DIM,
                        scale_lo=q_scale, scale_hi=q_scale)
        att = diff_attention(q, k_sh, v_sh, diff_lambda, subln_g, layer=j, batch=batch, seq=seq,
                             lambda_init=lambda_init)
        return matmul_residual(att, w_out, x, layer=j, cast=out_cast)

    y, w_ret_out, _, w13_d0 = retention_layer(0, w_in0, None, x, xb, (flat(ret_w_out), 0, 1),
                                               (flat(ffn_w13), 0, n_dense))
    x, xb = layer_norm(y, ln_g, ln_b, row=0)
    hidden, w2_d = matmul_swiglu(xb, w13_d0, cast=(flat(ffn_w2), 0, 1))
    w2_d = w2_d.reshape(ffn_w2.shape)
    y, w_ple = matmul_residual(hidden, w2_d, x, layer=0, cast=(flat(ple_w_gate), 0, 1))
    w_ple = w_ple.reshape(ple_w_gate.shape)
    x, xb = layer_norm(y, ln_g, ln_b, row=1)
    y, w_in1 = matmul_ple_residual(xb, w_ple, p, ple_w_proj, x, layer=0, cast=(flat(ret_w_in), 1, n_ret))
    x, xb = layer_norm(y, ln_g, ln_b, row=2)

    y, _, w13_m0, w2_m = retention_layer(1, w_in1, w_ret_out, x, xb, (flat(moe_w13), 0, n_moe),
                                         (flat(moe_w2), 0, 1))
    w13_m0 = w13_m0.reshape(moe_w13.shape[1:])
    w2_m = w2_m.reshape(n_moe, N_EXPERTS * D_EXPERT, D_MODEL)
    x, xb = layer_norm(y, ln_g, ln_b, row=3)
    gates = router_gates(xb, router_w, layer=0)
    hidden, w_kv = matmul_moe_swiglu(xb, w13_m0, gates, cast=(kv_w, 0, 1))
    y, w_q = matmul_residual(hidden, w2_m, x, layer=0, cast=(flat(diff_w_q), 0, 1))
    w_q = w_q.reshape(diff_w_q.shape)
    x, xb = layer_norm(y, ln_g, ln_b, row=4)
    y, w_att_out = matmul_ple_residual(xb, w_ple, p, ple_w_proj, x, layer=1, cast=(flat(diff_w_out), 0, 1))
    w_att_out = w_att_out.reshape(diff_w_out.shape)
    x, xb = layer_norm(y, ln_g, ln_b, row=5)
    k_sh = matmul_rope(xb, w_kv, cos_d, sin_d, col_off=0, n_out=kd, head_dim=DIFF_HEAD_DIM)
    v_sh = matmul(xb, w_kv, col_off=kd, n_out=vd, out_dtype=BF16)

    y, w13_d1 = attention_layer(0, w_q, w_att_out, k_sh, v_sh, x, xb, (flat(ffn_w13), 1, n_dense))
    x, xb = layer_norm(y, ln_g, ln_b, row=6)
    hidden, w13_m1 = matmul_swiglu(xb, w13_d1, cast=(flat(moe_w13), 1, n_moe))
    w13_m1 = w13_m1.reshape(moe_w13.shape[1:])
    y = matmul_residual(hidden, w2_d, x, layer=1)
    x, xb = layer_norm(y, ln_g, ln_b, row=7)
    y = matmul_ple_residual(xb, w_ple, p, ple_w_proj, x, layer=2)
    x, xb = layer_norm(y, ln_g, ln_b, row=8)

    y = attention_layer(1, w_q, w_att_out, k_sh, v_sh, x, xb, None)
    x, xb = layer_norm(y, ln_g, ln_b, row=9)
    gates = router_gates(xb, router_w, layer=1)
    hidden = matmul_moe_swiglu(xb, w13_m1, gates)
    y = matmul_residual(hidden, w2_m, x, layer=1)
    x, xb = layer_norm(y, ln_g, ln_b, row=10)
    y = matmul_ple_residual(xb, w_ple, p, ple_w_proj, x, layer=3)
    x, _ = layer_norm(y, ln_g, ln_b, row=11)
    return x.reshape(batch, seq, D_MODEL)
```

```python
import functools
import math

import jax
import jax.numpy as jnp
from jax import lax
from jax.experimental import pallas as pl
from jax.experimental.pallas import tpu as pltpu

F32 = jnp.float32
BF16 = jnp.bfloat16

D_MODEL = 4096
DEPTH = 4
N_A_LAYERS = DEPTH // 2
RET_HEADS = 16
RET_QK_DIM = D_MODEL // RET_HEADS
RET_V_DIM = 2 * D_MODEL // RET_HEADS
RET_CHUNK = 128
DIFF_HEADS = 16
DIFF_HEAD_DIM = D_MODEL // (2 * DIFF_HEADS)
DIFF_V_DIM = 2 * DIFF_HEAD_DIM
ROPE_THETA = 10000.0
D_FF = 2 * D_MODEL
N_EXPERTS = 8
TOP_K = 2
D_EXPERT = D_MODEL // 4
PLE_DIM = 256
LN_EPS = 1e-5
DEEPNORM_ALPHA = (2.0 * DEPTH) ** 0.25

V7X_LANES = 128
V7X_VMEM_BYTES = 64 * 1024 * 1024
V7X_VMEM_CAP = V7X_VMEM_BYTES - 6 * 1024 * 1024

MM_BM = 1024
MM_BM_WIDE_K = 512
MM_BN = 512
MM_BN_WIDE = 1024
LN_ROWS = 256
RET_ROWS = 1024
ATT_TILE = 512
ATT_STRIP = 64
LOG2_E = math.log2(math.e)


def _nbytes(shape, dtype):
    return math.prod(shape) * jnp.dtype(dtype).itemsize


def _compiler_params(semantics, block_bytes, temp_bytes):
    need = 2 * block_bytes + temp_bytes + (2 << 20)
    return pltpu.CompilerParams(
        dimension_semantics=semantics,
        vmem_limit_bytes=int(min(max(need, 16 << 20), V7X_VMEM_CAP)),
    )


def _w_spec(w, layer, k, bn, col_block_off=0):
    if w.ndim == 3:
        return pl.BlockSpec((None, k, bn), lambda i, j: (layer, 0, j + col_block_off))
    return pl.BlockSpec((k, bn), lambda i, j: (0, j + col_block_off))


def _hosting_cast(body, n_in):
    def hosted(*refs):
        body(*refs[:n_in], refs[n_in + 1])
        refs[n_in + 2][...] = refs[n_in][...].astype(BF16)
    return hosted


def _launch_matmul(body, name, grid, in_specs, args, out_block, out_shape, block_bytes, temp_bytes, cast=None):
    out_spec = pl.BlockSpec(out_block, lambda i, j: (i, j))
    semantics = ("parallel", "arbitrary")
    if cast is None:
        return pl.pallas_call(
            body, grid=grid, in_specs=in_specs, out_specs=out_spec, out_shape=out_shape,
            compiler_params=_compiler_params(semantics, block_bytes, temp_bytes), name=name,
        )(*args)
    src, part, n_parts = cast
    rows, cols = src.shape[0] // n_parts, src.shape[1]
    steps = grid[0] * grid[1]
    slab = rows // steps
    assert src.shape[0] % n_parts == 0 and rows % steps == 0 and slab % 16 == 0
    block_bytes += _nbytes((slab, cols), F32) + _nbytes((slab, cols), BF16)
    return pl.pallas_call(
        _hosting_cast(body, len(in_specs)),
        grid=grid,
        in_specs=list(in_specs) + [pl.BlockSpec((slab, cols), lambda i, j: (part * steps + i * grid[1] + j, 0))],
        out_specs=[out_spec, pl.BlockSpec((slab, cols), lambda i, j: (i * grid[1] + j, 0))],
        out_shape=[out_shape, jax.ShapeDtypeStruct((rows, cols), BF16)],
        compiler_params=_compiler_params(semantics, block_bytes, temp_bytes), name=name + "_cast",
    )(*args, src)


def _mm_plain_body(a_ref, w_ref, o_ref):
    o_ref[...] = jnp.dot(a_ref[...], w_ref[...], preferred_element_type=F32).astype(o_ref.dtype)


def matmul(a, w, *, layer=None, col_off=0, n_out, out_dtype, cast=None):
    m, k = a.shape
    bm, bn = (MM_BM, MM_BN_WIDE) if k <= D_MODEL else (MM_BM_WIDE_K, MM_BN)
    bm = min(bm, m)
    assert m % bm == 0 and n_out % bn == 0 and col_off % bn == 0
    blocks = _nbytes((bm, k), a.dtype) + _nbytes((k, bn), w.dtype) + _nbytes((bm, bn), out_dtype)
    return _launch_matmul(
        _mm_plain_body, "matmul", (m // bm, n_out // bn),
        [pl.BlockSpec((bm, k), lambda i, j: (i, 0)), _w_spec(w, layer, k, bn, col_off // bn)], (a, w),
        (bm, bn), jax.ShapeDtypeStruct((m, n_out), out_dtype), blocks, 2 * _nbytes((bm, bn), F32), cast)


def _mm_residual_body(a_ref, w_ref, x_ref, o_ref):
    o_ref[...] = DEEPNORM_ALPHA * x_ref[...] + jnp.dot(a_ref[...], w_ref[...], preferred_element_type=F32)


def matmul_residual(a, w, x, *, layer, cast=None):
    m, k = a.shape
    n = w.shape[-1]
    bm, bn = (MM_BM, MM_BN) if k <= D_MODEL else (MM_BM_WIDE_K, MM_BN)
    bm = min(bm, m)
    assert m % bm == 0 and n % bn == 0 and x.shape == (m, n)
    blocks = _nbytes((bm, k), a.dtype) + _nbytes((k, bn), w.dtype) + 2 * _nbytes((bm, bn), F32)
    return _launch_matmul(
        _mm_residual_body, "matmul_residual", (m // bm, n // bn),
        [pl.BlockSpec((bm, k), lambda i, j: (i, 0)), _w_spec(w, layer, k, bn),
         pl.BlockSpec((bm, bn), lambda i, j: (i, j))], (a, w, x),
        (bm, bn), jax.ShapeDtypeStruct((m, n), F32), blocks, 2 * _nbytes((bm, bn), F32), cast)


def _mm_rope_body(a_ref, w_ref, cos_ref, sin_ref, o_ref, *, head_dim, split_block, scale_lo, scale_hi):
    acc = jnp.dot(a_ref[...], w_ref[...], preferred_element_type=F32)
    cos = cos_ref[...]
    sin = sin_ref[...]
    if scale_lo == scale_hi:
        scale = scale_lo
    else:
        scale = jnp.where(pl.program_id(1) < split_block, scale_lo, scale_hi)
    for h in range(acc.shape[1] // head_dim):
        cols = slice(h * head_dim, (h + 1) * head_dim)
        xb = acc[:, cols]
        rot = pltpu.roll(xb, head_dim // 2, axis=1)
        o_ref[:, cols] = ((xb * cos + rot * sin) * scale).astype(o_ref.dtype)


def matmul_rope(a, w, cos_full, sin_signed, *, layer=None, col_off=0, n_out, head_dim,
                split_col=0, scale_lo=1.0, scale_hi=1.0, bm=MM_BM, bn=MM_BN_WIDE):
    m, k = a.shape
    seq = cos_full.shape[0]
    bm = min(bm, seq)
    assert m % bm == 0 and seq % bm == 0 and n_out % bn == 0 and bn % head_dim == 0
    assert col_off % bn == 0 and split_col % bn == 0
    pos_tiles = seq // bm
    blocks = (_nbytes((bm, k), a.dtype) + _nbytes((k, bn), w.dtype) + _nbytes((bm, bn), BF16)
              + 2 * _nbytes((bm, head_dim), F32))
    body = functools.partial(_mm_rope_body, head_dim=head_dim, split_block=split_col // bn,
                             scale_lo=scale_lo, scale_hi=scale_hi)
    return _launch_matmul(
        body, "matmul_rope", (m // bm, n_out // bn),
        [pl.BlockSpec((bm, k), lambda i, j: (i, 0)),
         _w_spec(w, layer, k, bn, col_off // bn),
         pl.BlockSpec((bm, head_dim), lambda i, j: (i % pos_tiles, 0)),
         pl.BlockSpec((bm, head_dim), lambda i, j: (i % pos_tiles, 0))], (a, w, cos_full, sin_signed),
        (bm, bn), jax.ShapeDtypeStruct((m, n_out), BF16), blocks, 3 * _nbytes((bm, bn), F32))


def _mm_swiglu_body(a_ref, wa_ref, wb_ref, o_ref):
    a = a_ref[...]
    ga = jnp.dot(a, wa_ref[...], preferred_element_type=F32)
    gb = jnp.dot(a, wb_ref[...], preferred_element_type=F32)
    o_ref[...] = (jax.nn.silu(ga) * gb).astype(o_ref.dtype)


def matmul_swiglu(a, w13, *, layer=None, bm=MM_BM, bn=MM_BN, cast=None):
    m, k = a.shape
    f = w13.shape[-1] // 2
    bm = min(bm, m)
    assert m % bm == 0 and f % bn == 0
    blocks = _nbytes((bm, k), a.dtype) + 2 * _nbytes((k, bn), w13.dtype) + _nbytes((bm, bn), BF16)
    return _launch_matmul(
        _mm_swiglu_body, "matmul_swiglu", (m // bm, f // bn),
        [pl.BlockSpec((bm, k), lambda i, j: (i, 0)), _w_spec(w13, layer, k, bn, 0),
         _w_spec(w13, layer, k, bn, f // bn)], (a, w13, w13),
        (bm, bn), jax.ShapeDtypeStruct((m, f), BF16), blocks, 4 * _nbytes((bm, bn), F32), cast)


def _mm_moe_swiglu_body(a_ref, wa_ref, wb_ref, gates_ref, o_ref, *, blocks_per_expert):
    a = a_ref[...]
    ga = jnp.dot(a, wa_ref[...], preferred_element_type=F32)
    gb = jnp.dot(a, wb_ref[...], preferred_element_type=F32)
    expert = pl.program_id(1) // blocks_per_expert
    gates = gates_ref[...]
    lane = lax.broadcasted_iota(jnp.int32, gates.shape, 1)
    gate = jnp.sum(jnp.where(lane == expert, gates, 0.0), axis=1, keepdims=True)
    o_ref[...] = (gate * (jax.nn.silu(ga) * gb)).astype(o_ref.dtype)


def matmul_moe_swiglu(a, w13, gates, *, bm=MM_BM, bn=MM_BN, cast=None):
    m, k = a.shape
    n_exp, f = w13.shape[0], w13.shape[-1] // 2
    bm = min(bm, m)
    assert m % bm == 0 and f % bn == 0
    bpe = f // bn
    blocks = (_nbytes((bm, k), a.dtype) + 2 * _nbytes((k, bn), w13.dtype) + _nbytes((bm, bn), BF16)
              + _nbytes((bm, V7X_LANES), F32))
    return _launch_matmul(
        functools.partial(_mm_moe_swiglu_body, blocks_per_expert=bpe), "matmul_moe_swiglu",
        (m // bm, n_exp * bpe),
        [pl.BlockSpec((bm, k), lambda i, j: (i, 0)),
         pl.BlockSpec((None, k, bn), lambda i, j: (j // bpe, 0, j % bpe)),
         pl.BlockSpec((None, k, bn), lambda i, j: (j // bpe, 0, j % bpe + bpe)),
         pl.BlockSpec((bm, V7X_LANES), lambda i, j: (i, 0))], (a, w13, w13, gates),
        (bm, bn), jax.ShapeDtypeStruct((m, n_exp * f), BF16), blocks, 4 * _nbytes((bm, bn), F32), cast)


def _mm_ple_body(a_ref, wg_ref, p_ref, wp_ref, x_ref, o_ref):
    gate = jnp.dot(a_ref[...], wg_ref[...], preferred_element_type=F32)
    proj = jnp.dot(p_ref[...].astype(BF16), wp_ref[...], preferred_element_type=F32)
    o_ref[...] = DEEPNORM_ALPHA * x_ref[...] + jax.nn.sigmoid(gate) * proj


def matmul_ple_residual(a, w_gate, p, w_proj, x, *, layer, bm=MM_BM, bn=MM_BN, cast=None):
    m, k = a.shape
    n = w_gate.shape[-1]
    kp = p.shape[-1]
    bm = min(bm, m)
    assert m % bm == 0 and n % bn == 0 and x.shape == (m, n)
    blocks = (_nbytes((bm, k), a.dtype) + _nbytes((k, bn), w_gate.dtype) + _nbytes((bm, kp), p.dtype)
              + _nbytes((kp, bn), w_proj.dtype) + 2 * _nbytes((bm, bn), F32))
    return _launch_matmul(
        _mm_ple_body, "matmul_ple_residual", (m // bm, n // bn),
        [pl.BlockSpec((bm, k), lambda i, j: (i, 0)),
         _w_spec(w_gate, layer, k, bn),
         pl.BlockSpec((None, bm, kp), lambda i, j: (layer, i, 0)),
         _w_spec(w_proj, layer, kp, bn),
         pl.BlockSpec((bm, bn), lambda i, j: (i, j))], (a, w_gate, p, w_proj, x),
        (bm, bn), jax.ShapeDtypeStruct((m, n), F32), blocks, 4 * _nbytes((bm, bn), F32), cast)


def _ln_body(y_ref, g_ref, b_ref, o_ref, ob_ref):
    y = y_ref[...]
    mu = jnp.mean(y, axis=-1, keepdims=True)
    d = y - mu
    var = jnp.mean(d * d, axis=-1, keepdims=True)
    out = d * lax.rsqrt(var + LN_EPS) * g_ref[...] + b_ref[...]
    o_ref[...] = out
    ob_ref[...] = out.astype(BF16)


def layer_norm(y, ln_g, ln_b, *, row, rows=LN_ROWS):
    m, d = y.shape
    rows = min(rows, m)
    assert m % rows == 0
    blocks = 2 * _nbytes((rows, d), F32) + _nbytes((rows, d), BF16) + 2 * _nbytes((1, d), F32)
    tile = pl.BlockSpec((rows, d), lambda i: (i, 0))
    vec = pl.BlockSpec((None, 1, d), lambda i: (row, 0, 0))
    return pl.pallas_call(
        _ln_body,
        grid=(m // rows,),
        in_specs=[tile, vec, vec],
        out_specs=[tile, tile],
        out_shape=[jax.ShapeDtypeStruct((m, d), F32), jax.ShapeDtypeStruct((m, d), BF16)],
        compiler_params=_compiler_params(("parallel",), blocks, 3 * _nbytes((rows, d), F32)),
        name="layer_norm",
    )(y, ln_g, ln_b)


def _retention_body(q_ref, k_ref, v_ref, g_ref, dmask_ref, qdec_ref, kdec_ref, cdec_ref, o_ref, state_ref,
                    *, n_chunks, chunk):
    @pl.when(pl.program_id(2) == 0)
    def _():
        state_ref[...] = jnp.zeros_like(state_ref)

    dmask = dmask_ref[...]
    qdec = qdec_ref[...]
    kdec = kdec_ref[...]
    cdec = cdec_ref[...]
    for c in range(n_chunks):
        rows = pl.ds(c * chunk, chunk)
        q = q_ref[rows, :]
        k = k_ref[rows, :]
        v = v_ref[rows, :]
        state = state_ref[...]
        scores = lax.dot_general(q, k, (((1,), (1,)), ((), ())), preferred_element_type=F32) * dmask
        inner = jnp.dot(scores.astype(BF16), v, preferred_element_type=F32)
        cross = jnp.dot((q.astype(F32) * qdec).astype(BF16), state.astype(BF16), preferred_element_type=F32)
        k_scaled = (k.astype(F32) * kdec).astype(BF16)
        update = lax.dot_general(k_scaled, v, (((0,), (0,)), ((), ())), preferred_element_type=F32)
        state_ref[...] = state * cdec + update
        out = inner + cross
        mu = jnp.mean(out, axis=-1, keepdims=True)
        d = out - mu
        var = jnp.mean(d * d, axis=-1, keepdims=True)
        normed = d * lax.rsqrt(var + LN_EPS)
        gate = g_ref[rows, :].astype(F32)
        o_ref[rows, :] = (jax.nn.silu(gate) * normed).astype(o_ref.dtype)


def retention_tables():
    h, c = RET_HEADS, RET_CHUNK
    log_gamma = jnp.log1p(-jnp.exp2(-5.0 - jnp.arange(h, dtype=F32)))
    pos = jnp.arange(c, dtype=F32)
    rel = pos[:, None] - pos[None, :]
    dmask = jnp.where(rel >= 0, jnp.exp(log_gamma[:, None, None] * jnp.maximum(rel, 0.0)), 0.0)
    qdec = jnp.exp(log_gamma[:, None] * (pos + 1.0))[:, :, None]
    kdec = jnp.exp(log_gamma[:, None] * (c - 1.0 - pos))[:, :, None]
    cdec = jnp.exp(log_gamma * c)[:, None, None]
    return (dmask,
            jnp.broadcast_to(qdec, (h, c, RET_QK_DIM)),
            jnp.broadcast_to(kdec, (h, c, RET_QK_DIM)),
            jnp.broadcast_to(cdec, (h, 1, RET_V_DIM)))


def retention(qk, vg, tables, *, batch, seq, rows=RET_ROWS):
    h, dk, dv, c = RET_HEADS, RET_QK_DIM, RET_V_DIM, RET_CHUNK
    rows = min(rows, seq)
    assert seq % rows == 0 and rows % c == 0
    steps = seq // rows
    dmask, qdec, kdec, cdec = tables
    row_tile = lambda b, hh, l: b * steps + l
    blocks = (2 * _nbytes((rows, dk), BF16) + 3 * _nbytes((rows, dv), BF16) + _nbytes((c, c), F32)
              + 2 * _nbytes((c, dk), F32) + _nbytes((8, dv), F32))
    return pl.pallas_call(
        functools.partial(_retention_body, n_chunks=rows // c, chunk=c),
        grid=(batch, h, steps),
        in_specs=[pl.BlockSpec((rows, dk), lambda b, hh, l: (row_tile(b, hh, l), hh)),
                  pl.BlockSpec((rows, dk), lambda b, hh, l: (row_tile(b, hh, l), h + hh)),
                  pl.BlockSpec((rows, dv), lambda b, hh, l: (row_tile(b, hh, l), hh)),
                  pl.BlockSpec((rows, dv), lambda b, hh, l: (row_tile(b, hh, l), h + hh)),
                  pl.BlockSpec((None, c, c), lambda b, hh, l: (hh, 0, 0)),
                  pl.BlockSpec((None, c, dk), lambda b, hh, l: (hh, 0, 0)),
                  pl.BlockSpec((None, c, dk), lambda b, hh, l: (hh, 0, 0)),
                  pl.BlockSpec((None, 1, dv), lambda b, hh, l: (hh, 0, 0))],
        out_specs=pl.BlockSpec((rows, dv), lambda b, hh, l: (row_tile(b, hh, l), hh)),
        out_shape=jax.ShapeDtypeStruct((batch * seq, h * dv), BF16),
        scratch_shapes=[pltpu.VMEM((dk, dv), F32)],
        compiler_params=_compiler_params(("parallel", "parallel", "arbitrary"), blocks,
                                         _nbytes((dk, dv), F32) * 4 + _nbytes((c, dv), F32) * 8),
        name="retention",
    )(qk, qk, vg, vg, dmask, qdec, kdec, cdec)


def _diff_attention_body(q_ref, k_ref, v_ref, lam_ref, g_ref, o_ref, max_ref, sum_ref, acc_ref, s_ref, p_ref,
                         *, tile, strip, lambda_init):
    qi = pl.program_id(2)
    d = DIFF_HEAD_DIM
    lanes = V7X_LANES
    contract_last = (((1,), (1,)), ((), ()))

    def scores(kt, width):
        rows_k = pl.ds(pl.multiple_of(kt * tile, tile), width)
        for c in range(2):
            s_ref[c, :, :width] = lax.dot_general(q_ref[:, c * d:(c + 1) * d], k_ref[rows_k, c * d:(c + 1) * d],
                                                  contract_last, preferred_element_type=F32)
        return rows_k

    def block(c, r, cb, diagonal):
        if diagonal and cb * lanes >= (r + 1) * strip:
            return None
        s = s_ref[c, r * strip:(r + 1) * strip, cb * lanes:(cb + 1) * lanes]
        if diagonal and (cb + 1) * lanes - 1 > r * strip:
            row_id = r * strip + lax.broadcasted_iota(jnp.int32, (strip, lanes), 0)
            col_id = cb * lanes + lax.broadcasted_iota(jnp.int32, (strip, lanes), 1)
            s = jnp.where(row_id >= col_id, s, -jnp.inf)
        return s

    def max_sweep(kt, width, diagonal):
        scores(kt, width)
        for c in range(2):
            for r in range(tile // strip):
                rows = slice(r * strip, (r + 1) * strip)
                blocks = [block(c, r, cb, diagonal) for cb in range(width // lanes)]
                max_ref[c, rows] = functools.reduce(jnp.maximum,
                                                    [max_ref[c, rows]] + [b for b in blocks if b is not None])

    def sum_sweep(kt, width, diagonal):
        rows_k = scores(kt, width)
        for c in range(2):
            for r in range(tile // strip):
                rows = slice(r * strip, (r + 1) * strip)
                row_max = max_ref[c, rows]
                total = sum_ref[c, rows]
                for cb in range(width // lanes):
                    s = block(c, r, cb, diagonal)
                    if s is None:
                        p_ref[c, rows, cb * lanes:(cb + 1) * lanes] = jnp.zeros((strip, lanes), BF16)
                        continue
                    p = jnp.exp2(s - row_max)
                    total = total + p
                    p_ref[c, rows, cb * lanes:(cb + 1) * lanes] = p.astype(BF16)
                sum_ref[c, rows] = total
        v = v_ref[rows_k, :]
        for c in range(2):
            acc_ref[c] += jnp.dot(p_ref[c, :, :width], v, preferred_element_type=F32)

    def loop(sweep):
        def pair(n, carry):
            sweep(2 * n, 2 * tile, False)
            return carry
        lax.fori_loop(0, qi // 2, pair, 0)

        @pl.when(qi % 2 == 1)
        def _():
            sweep(qi - 1, tile, False)
        sweep(qi, tile, True)

    max_ref[...] = jnp.full_like(max_ref, -jnp.inf)
    sum_ref[...] = jnp.zeros_like(sum_ref)
    acc_ref[...] = jnp.zeros_like(acc_ref)
    loop(max_sweep)
    for c in range(2):
        max_ref[c] = jnp.broadcast_to(jnp.max(max_ref[c], axis=1, keepdims=True), (tile, lanes))
    loop(sum_sweep)

    lam = lam_ref[...]
    lam_full = (jnp.exp(jnp.sum(lam[0:1] * lam[1:2], axis=1, keepdims=True))
                - jnp.exp(jnp.sum(lam[2:3] * lam[3:4], axis=1, keepdims=True)) + lambda_init)
    inv_l = [1.0 / jnp.sum(sum_ref[c], axis=1, keepdims=True) for c in range(2)]
    out = acc_ref[0] * inv_l[0] - lam_full * (acc_ref[1] * inv_l[1])
    y = out * lax.rsqrt(jnp.mean(out * out, axis=-1, keepdims=True) + LN_EPS)
    o_ref[...] = ((y * g_ref[...]) * (1.0 - lambda_init)).astype(o_ref.dtype)


def diff_attention(q, k, v, lam, subln_g, *, layer, batch, seq, lambda_init, tile=ATT_TILE, strip=ATT_STRIP):
    h, dv = DIFF_HEADS, DIFF_V_DIM
    tile = min(tile, seq)
    assert seq % tile == 0 and tile % strip == 0
    nq = seq // tile
    blocks = (2 * _nbytes((tile, dv), BF16) + 2 * _nbytes((seq, dv), BF16) + _nbytes((8, V7X_LANES), F32)
              + _nbytes((8, dv), F32))
    scratch = [pltpu.VMEM((2, tile, V7X_LANES), F32), pltpu.VMEM((2, tile, V7X_LANES), F32),
               pltpu.VMEM((2, tile, dv), F32), pltpu.VMEM((2, tile, 2 * tile), F32),
               pltpu.VMEM((2, tile, 2 * tile), BF16)]
    scratch_bytes = (2 * 2 * _nbytes((tile, V7X_LANES), F32) + 2 * _nbytes((tile, dv), F32)
                     + 2 * _nbytes((tile, 2 * tile), F32) + 2 * _nbytes((tile, 2 * tile), BF16))
    return pl.pallas_call(
        functools.partial(_diff_attention_body, tile=tile, strip=strip, lambda_init=lambda_init),
        grid=(batch, h, nq),
        in_specs=[pl.BlockSpec((tile, dv), lambda b, hh, qi: (b * nq + qi, hh)),
                  pl.BlockSpec((seq, dv), lambda b, hh, qi: (b, hh)),
                  pl.BlockSpec((seq, dv), lambda b, hh, qi: (b, hh)),
                  pl.BlockSpec((None, 4, DIFF_HEAD_DIM), lambda b, hh, qi: (layer, 0, 0)),
                  pl.BlockSpec((None, 1, dv), lambda b, hh, qi: (layer, 0, 0))],
        out_specs=pl.BlockSpec((tile, dv), lambda b, hh, qi: (b * nq + qi, hh)),
        out_shape=jax.ShapeDtypeStruct((batch * seq, h * dv), BF16),
        scratch_shapes=scratch,
        compiler_params=_compiler_params(("parallel", "parallel", "arbitrary"), blocks,
                                         scratch_bytes + 2 * _nbytes((tile, dv), F32)),
        name="diff_attention",
    )(q, k, v, lam, subln_g)


def _router_body(a_ref, w_ref, o_ref):
    logits = jnp.dot(a_ref[...], w_ref[...], preferred_element_type=F32)
    lane = lax.broadcasted_iota(jnp.int32, logits.shape, 1).astype(F32)
    logits = jnp.where(lane < N_EXPERTS, logits, -jnp.inf)
    top1 = jnp.max(logits, axis=1, keepdims=True)
    idx1 = jnp.min(jnp.where(logits == top1, lane, float(V7X_LANES)), axis=1, keepdims=True)
    rest = jnp.where(lane == idx1, -jnp.inf, logits)
    top2 = jnp.max(rest, axis=1, keepdims=True)
    idx2 = jnp.min(jnp.where(rest == top2, lane, float(V7X_LANES)), axis=1, keepdims=True)
    e = jnp.exp(top2 - top1)
    w1 = 1.0 / (1.0 + e)
    w2 = e / (1.0 + e)
    o_ref[...] = jnp.where(lane == idx1, w1, 0.0) + jnp.where(lane == idx2, w2, 0.0)


def router_gates(a, w_router_padded, *, layer, bm=MM_BM):
    m, k = a.shape
    bm = min(bm, m)
    assert m % bm == 0
    blocks = _nbytes((bm, k), a.dtype) + _nbytes((k, V7X_LANES), BF16) + _nbytes((bm, V7X_LANES), F32)
    return pl.pallas_call(
        _router_body,
        grid=(m // bm,),
        in_specs=[pl.BlockSpec((bm, k), lambda i: (i, 0)),
                  pl.BlockSpec((None, k, V7X_LANES), lambda i: (layer, 0, 0))],
        out_specs=pl.BlockSpec((bm, V7X_LANES), lambda i: (i, 0)),
        out_shape=jax.ShapeDtypeStruct((m, V7X_LANES), F32),
        compiler_params=_compiler_params(("parallel",), blocks, 8 * _nbytes((bm, V7X_LANES), F32)),
        name="router_gates",
    )(a, w_router_padded)


def _rope_tables(seq, dim):
    inv_freq = 1.0 / (ROPE_THETA ** (jnp.arange(0, dim, 2, dtype=F32) / dim))
    ang = jnp.arange(seq, dtype=F32)[:, None] * inv_freq[None, :]
    cos, sin = jnp.cos(ang), jnp.sin(ang)
    return jnp.concatenate([cos, cos], axis=-1), jnp.concatenate([-sin, sin], axis=-1)


def kernel(x, p, ret_w_in, ret_w_out, kv_w, diff_w_q, diff_w_out, diff_lambda, diff_subln_g, ffn_w13, ffn_w2,
           moe_router, moe_w13, moe_w2, ple_w_gate, ple_w_proj, ln_g, ln_b):
    batch, seq, d = x.shape
    assert d == D_MODEL
    tokens = batch * seq
    cos_r, sin_r = _rope_tables(seq, RET_QK_DIM)
    cos_d, sin_d = _rope_tables(seq, DIFF_HEAD_DIM)
    ret_tabs = retention_tables()

    w_in0 = ret_w_in[0].astype(BF16)
    ple_w_proj = ple_w_proj.astype(BF16)
    router_w = jnp.pad(moe_router.astype(BF16), ((0, 0), (0, 0), (0, V7X_LANES - N_EXPERTS)))
    flat = lambda w: w.reshape(-1, w.shape[-1])
    n_ret, n_dense, n_moe = ret_w_out.shape[0], ffn_w13.shape[0], moe_w13.shape[0]
    ln_g = ln_g.reshape(DEPTH * 3, 1, D_MODEL)
    ln_b = ln_b.reshape(DEPTH * 3, 1, D_MODEL)
    subln_g = diff_subln_g.reshape(-1, 1, DIFF_V_DIM)
    p = p.reshape(DEPTH, tokens, PLE_DIM)

    x = x.reshape(tokens, D_MODEL)
    xb = x.astype(BF16)
    qk_w = RET_HEADS * RET_QK_DIM
    v_w = RET_HEADS * RET_V_DIM
    kd = 2 * DIFF_HEADS * DIFF_HEAD_DIM
    vd = DIFF_HEADS * DIFF_V_DIM
    q_scale = DIFF_HEAD_DIM ** -0.5 * LOG2_E

    def retention_layer(i, w_in, w_out, x, xb, vg_cast, out_cast):
        qk = matmul_rope(xb, w_in, cos_r, sin_r, col_off=0, n_out=2 * qk_w, head_dim=RET_QK_DIM,
                         split_col=qk_w, scale_lo=1.0, scale_hi=RET_QK_DIM ** -0.5)
        vg, cast_a = matmul(xb, w_in, col_off=2 * qk_w, n_out=2 * v_w, out_dtype=BF16, cast=vg_cast)
        w_out = w_out if w_out is not None else cast_a.reshape(ret_w_out.shape)
        gated = retention(qk, vg, ret_tabs, batch=batch, seq=seq)
        y, cast_b = matmul_residual(gated, w_out, x, layer=i, cast=out_cast)
        return y, w_out, cast_a, cast_b

    def attention_layer(j, w_q, w_out, k_sh, v_sh, x, xb, out_cast):
        lambda_init = 0.8 - 0.6 * math.exp(-0.3 * (j + N_A_LAYERS))
        q = matmul_rope(xb, w_q, cos_d, sin_d, layer=j, n_out=kd, head_dim=DIFF_HEAD_DIM,
                        scale_lo=q_scale, scale_hi=q_scale)
        att = diff_attention(q, k_sh, v_sh, diff_lambda, subln_g, layer=j, batch=batch, seq=seq,
                             lambda_init=lambda_init)
        return matmul_residual(att, w_out, x, layer=j, cast=out_cast)

    y, w_ret_out, _, w13_d0 = retention_layer(0, w_in0, None, x, xb, (flat(ret_w_out), 0, 1),
                                               (flat(ffn_w13), 0, n_dense))
    x, xb = layer_norm(y, ln_g, ln_b, row=0)
    hidden, w2_d = matmul_swiglu(xb, w13_d0, cast=(flat(ffn_w2), 0, 1))
    w2_d = w2_d.reshape(ffn_w2.shape)
    y, w_ple = matmul_residual(hidden, w2_d, x, layer=0, cast=(flat(ple_w_gate), 0, 1))
    w_ple = w_ple.reshape(ple_w_gate.shape)
    x, xb = layer_norm(y, ln_g, ln_b, row=1)
    y, w_in1 = matmul_ple_residual(xb, w_ple, p, ple_w_proj, x, layer=0, cast=(flat(ret_w_in), 1, n_ret))
    x, xb = layer_norm(y, ln_g, ln_b, row=2)

    y, _, w13_m0, w2_m = retention_layer(1, w_in1, w_ret_out, x, xb, (flat(moe_w13), 0, n_moe),
                                         (flat(moe_w2), 0, 1))
    w13_m0 = w13_m0.reshape(moe_w13.shape[1:])
    w2_m = w2_m.reshape(n_moe, N_EXPERTS * D_EXPERT, D_MODEL)
    x, xb = layer_norm(y, ln_g, ln_b, row=3)
    gates = router_gates(xb, router_w, layer=0)
    hidden, w_kv = matmul_moe_swiglu(xb, w13_m0, gates, cast=(kv_w, 0, 1))
    y, w_q = matmul_residual(hidden, w2_m, x, layer=0, cast=(flat(diff_w_q), 0, 1))
    w_q = w_q.reshape(diff_w_q.shape)
    x, xb = layer_norm(y, ln_g, ln_b, row=4)
    y, w_att_out = matmul_ple_residual(xb, w_ple, p, ple_w_proj, x, layer=1, cast=(flat(diff_w_out), 0, 1))
    w_att_out = w_att_out.reshape(diff_w_out.shape)
    x, xb = layer_norm(y, ln_g, ln_b, row=5)
    k_sh = matmul_rope(xb, w_kv, cos_d, sin_d, col_off=0, n_out=kd, head_dim=DIFF_HEAD_DIM)
    v_sh = matmul(xb, w_kv, col_off=kd, n_out=vd, out_dtype=BF16)

    y, w13_d1 = attention_layer(0, w_q, w_att_out, k_sh, v_sh, x, xb, (flat(ffn_w13), 1, n_dense))
    x, xb = layer_norm(y, ln_g, ln_b, row=6)
    hidden, w13_m1 = matmul_swiglu(xb, w13_d1, cast=(flat(moe_w13), 1, n_moe))
    w13_m1 = w13_m1.reshape(moe_w13.shape[1:])
    y = matmul_residual(hidden, w2_d, x, layer=1)
    x, xb = layer_norm(y, ln_g, ln_b, row=7)
    y = matmul_ple_residual(xb, w_ple, p, ple_w_proj, x, layer=2)
    x, xb = layer_norm(y, ln_g, ln_b, row=8)

    y = attention_layer(1, w_q, w_att_out, k_sh, v_sh, x, xb, None)
    x, xb = layer_norm(y, ln_g, ln_b, row=9)
    gates = router_gates(xb, router_w, layer=1)
    hidden = matmul_moe_swiglu(xb, w13_m1, gates)
    y = matmul_residual(hidden, w2_m, x, layer=1)
    x, xb = layer_norm(y, ln_g, ln_b, row=10)
    y = matmul_ple_residual(xb, w_ple, p, ple_w_proj, x, layer=3)
    x, _ = layer_norm(y, ln_g, ln_b, row=11)
    return x.reshape(batch, seq, D_MODEL)
```

```python
import functools
import math
from typing import NamedTuple

import jax
import jax.numpy as jnp
from jax import lax
from jax.experimental import pallas as pl
from jax.experimental.pallas import tpu as pltpu

F32 = jnp.float32
BF16 = jnp.bfloat16

D_MODEL = 4096
DEPTH = 4
N_A_LAYERS = DEPTH // 2
RET_HEADS = 16
RET_QK_DIM = D_MODEL // RET_HEADS
RET_V_DIM = 2 * D_MODEL // RET_HEADS
RET_CHUNK = 128
DIFF_HEADS = 16
DIFF_HEAD_DIM = D_MODEL // (2 * DIFF_HEADS)
DIFF_V_DIM = 2 * DIFF_HEAD_DIM
ROPE_THETA = 10000.0
D_FF = 2 * D_MODEL
N_EXPERTS = 8
TOP_K = 2
D_EXPERT = D_MODEL // 4
PLE_DIM = 256
LN_EPS = 1e-5
DEEPNORM_ALPHA = (2.0 * DEPTH) ** 0.25

V7X_LANES = 128
V7X_VMEM_BYTES = 64 * 1024 * 1024
V7X_VMEM_CAP = V7X_VMEM_BYTES - 6 * 1024 * 1024

MM_BM = 1024
MM_BM_WIDE_K = 512
MM_BN = 512
MM_BN_WIDE = 1024
LN_ROWS = 256
RET_ROWS = 1024
ATT_TILE = 512
ATT_STRIP = 64
LOG2_E = math.log2(math.e)


def _nbytes(shape, dtype):
    return math.prod(shape) * jnp.dtype(dtype).itemsize


def _compiler_params(semantics, block_bytes, temp_bytes):
    need = 2 * block_bytes + temp_bytes + (2 << 20)
    return pltpu.CompilerParams(
        dimension_semantics=semantics,
        vmem_limit_bytes=int(min(max(need, 16 << 20), V7X_VMEM_CAP)),
    )


def _w_spec(w, layer, k, bn, col_block_off=0):
    if w.ndim == 3:
        return pl.BlockSpec((None, k, bn), lambda i, j: (layer, 0, j + col_block_off))
    return pl.BlockSpec((k, bn), lambda i, j: (0, j + col_block_off))


def _hosting_cast(body, n_in):
    def hosted(*refs):
        body(*refs[:n_in], refs[n_in + 1])
        refs[n_in + 2][...] = refs[n_in][...].astype(BF16)
    return hosted


def _launch_matmul(body, name, grid, in_specs, args, out_block, out_shape, block_bytes, temp_bytes, cast=None):
    out_spec = pl.BlockSpec(out_block, lambda i, j: (i, j))
    semantics = ("parallel", "arbitrary")
    if cast is None:
        return pl.pallas_call(
            body, grid=grid, in_specs=in_specs, out_specs=out_spec, out_shape=out_shape,
            compiler_params=_compiler_params(semantics, block_bytes, temp_bytes), name=name,
        )(*args)
    src, part, n_parts = cast
    rows, cols = src.shape[0] // n_parts, src.shape[1]
    steps = grid[0] * grid[1]
    slab = rows // steps
    assert src.shape[0] % n_parts == 0 and rows % steps == 0 and slab % 16 == 0
    block_bytes += _nbytes((slab, cols), F32) + _nbytes((slab, cols), BF16)
    return pl.pallas_call(
        _hosting_cast(body, len(in_specs)),
        grid=grid,
        in_specs=list(in_specs) + [pl.BlockSpec((slab, cols), lambda i, j: (part * steps + i * grid[1] + j, 0))],
        out_specs=[out_spec, pl.BlockSpec((slab, cols), lambda i, j: (i * grid[1] + j, 0))],
        out_shape=[out_shape, jax.ShapeDtypeStruct((rows, cols), BF16)],
        compiler_params=_compiler_params(semantics, block_bytes, temp_bytes), name=name + "_cast",
    )(*args, src)


def _mm_plain_body(a_ref, w_ref, o_ref):
    o_ref[...] = jnp.dot(a_ref[...], w_ref[...], preferred_element_type=F32).astype(o_ref.dtype)


def matmul(a, w, *, layer=None, col_off=0, n_out, out_dtype, cast=None):
    m, k = a.shape
    bm, bn = (MM_BM, MM_BN_WIDE) if k <= D_MODEL else (MM_BM_WIDE_K, MM_BN)
    bm = min(bm, m)
    assert m % bm == 0 and n_out % bn == 0 and col_off % bn == 0
    blocks = _nbytes((bm, k), a.dtype) + _nbytes((k, bn), w.dtype) + _nbytes((bm, bn), out_dtype)
    return _launch_matmul(
        _mm_plain_body, "matmul", (m // bm, n_out // bn),
        [pl.BlockSpec((bm, k), lambda i, j: (i, 0)), _w_spec(w, layer, k, bn, col_off // bn)], (a, w),
        (bm, bn), jax.ShapeDtypeStruct((m, n_out), out_dtype), blocks, 2 * _nbytes((bm, bn), F32), cast)


class Normed(NamedTuple):
    y: jax.Array
    mean: jax.Array
    rstd: jax.Array
    gain: jax.Array
    bias: jax.Array
    row: int


def _residual_operands(res, bm, bn):
    tile = pl.BlockSpec((bm, bn), lambda i, j: (i, j))
    if not isinstance(res, Normed):
        return [tile], (res,), _nbytes((bm, bn), F32)
    stat = pl.BlockSpec((bm, V7X_LANES), lambda i, j: (i, 0))
    vec = pl.BlockSpec((None, 1, bn), lambda i, j: (res.row, 0, j))
    nbytes = _nbytes((bm, bn), F32) + 2 * _nbytes((bm, V7X_LANES), F32) + 2 * _nbytes((8, bn), F32)
    return [tile, stat, stat, vec, vec], (res.y, res.mean, res.rstd, res.gain, res.bias), nbytes


def _residual_tile(refs):
    if len(refs) == 1:
        return refs[0][...]
    y_ref, mean_ref, rstd_ref, gain_ref, bias_ref = refs
    reps = y_ref.shape[1] // V7X_LANES
    mean = jnp.concatenate([mean_ref[...]] * reps, axis=1)
    rstd = jnp.concatenate([rstd_ref[...]] * reps, axis=1)
    return (y_ref[...] - mean) * rstd * gain_ref[...] + bias_ref[...]


def _mm_residual_body(a_ref, w_ref, *refs):
    *res_refs, o_ref = refs
    o_ref[...] = (DEEPNORM_ALPHA * _residual_tile(res_refs)
                  + jnp.dot(a_ref[...], w_ref[...], preferred_element_type=F32))


def matmul_residual(a, w, res, *, layer, cast=None):
    m, k = a.shape
    n = w.shape[-1]
    bm, bn = (MM_BM, MM_BN) if k <= D_MODEL else (MM_BM_WIDE_K, MM_BN)
    bm = min(bm, m)
    assert m % bm == 0 and n % bn == 0
    res_specs, res_args, res_bytes = _residual_operands(res, bm, bn)
    blocks = _nbytes((bm, k), a.dtype) + _nbytes((k, bn), w.dtype) + _nbytes((bm, bn), F32) + res_bytes
    return _launch_matmul(
        _mm_residual_body, "matmul_residual", (m // bm, n // bn),
        [pl.BlockSpec((bm, k), lambda i, j: (i, 0)), _w_spec(w, layer, k, bn)] + res_specs, (a, w) + res_args,
        (bm, bn), jax.ShapeDtypeStruct((m, n), F32), blocks, 3 * _nbytes((bm, bn), F32), cast)


def _mm_rope_body(a_ref, w_ref, cos_ref, sin_ref, o_ref, *, head_dim, split_block, scale_lo, scale_hi):
    acc = jnp.dot(a_ref[...], w_ref[...], preferred_element_type=F32)
    cos = cos_ref[...]
    sin = sin_ref[...]
    if scale_lo == scale_hi:
        scale = scale_lo
    else:
        scale = jnp.where(pl.program_id(1) < split_block, scale_lo, scale_hi)
    for h in range(acc.shape[1] // head_dim):
        cols = slice(h * head_dim, (h + 1) * head_dim)
        xb = acc[:, cols]
        rot = pltpu.roll(xb, head_dim // 2, axis=1)
        o_ref[:, cols] = ((xb * cos + rot * sin) * scale).astype(o_ref.dtype)


def matmul_rope(a, w, cos_full, sin_signed, *, layer=None, col_off=0, n_out, head_dim,
                split_col=0, scale_lo=1.0, scale_hi=1.0, bm=MM_BM, bn=MM_BN_WIDE):
    m, k = a.shape
    seq = cos_full.shape[0]
    bm = min(bm, seq)
    assert m % bm == 0 and seq % bm == 0 and n_out % bn == 0 and bn % head_dim == 0
    assert col_off % bn == 0 and split_col % bn == 0
    pos_tiles = seq // bm
    blocks = (_nbytes((bm, k), a.dtype) + _nbytes((k, bn), w.dtype) + _nbytes((bm, bn), BF16)
              + 2 * _nbytes((bm, head_dim), F32))
    body = functools.partial(_mm_rope_body, head_dim=head_dim, split_block=split_col // bn,
                             scale_lo=scale_lo, scale_hi=scale_hi)
    return _launch_matmul(
        body, "matmul_rope", (m // bm, n_out // bn),
        [pl.BlockSpec((bm, k), lambda i, j: (i, 0)),
         _w_spec(w, layer, k, bn, col_off // bn),
         pl.BlockSpec((bm, head_dim), lambda i, j: (i % pos_tiles, 0)),
         pl.BlockSpec((bm, head_dim), lambda i, j: (i % pos_tiles, 0))], (a, w, cos_full, sin_signed),
        (bm, bn), jax.ShapeDtypeStruct((m, n_out), BF16), blocks, 3 * _nbytes((bm, bn), F32))


def _mm_swiglu_body(a_ref, wa_ref, wb_ref, o_ref):
    a = a_ref[...]
    ga = jnp.dot(a, wa_ref[...], preferred_element_type=F32)
    gb = jnp.dot(a, wb_ref[...], preferred_element_type=F32)
    o_ref[...] = (jax.nn.silu(ga) * gb).astype(o_ref.dtype)


def matmul_swiglu(a, w13, *, layer=None, bm=MM_BM, bn=MM_BN, cast=None):
    m, k = a.shape
    f = w13.shape[-1] // 2
    bm = min(bm, m)
    assert m % bm == 0 and f % bn == 0
    blocks = _nbytes((bm, k), a.dtype) + 2 * _nbytes((k, bn), w13.dtype) + _nbytes((bm, bn), BF16)
    return _launch_matmul(
        _mm_swiglu_body, "matmul_swiglu", (m // bm, f // bn),
        [pl.BlockSpec((bm, k), lambda i, j: (i, 0)), _w_spec(w13, layer, k, bn, 0),
         _w_spec(w13, layer, k, bn, f // bn)], (a, w13, w13),
        (bm, bn), jax.ShapeDtypeStruct((m, f), BF16), blocks, 4 * _nbytes((bm, bn), F32), cast)


def _mm_moe_swiglu_body(a_ref, wa_ref, wb_ref, gates_ref, o_ref, *, blocks_per_expert):
    a = a_ref[...]
    ga = jnp.dot(a, wa_ref[...], preferred_element_type=F32)
    gb = jnp.dot(a, wb_ref[...], preferred_element_type=F32)
    expert = pl.program_id(1) // blocks_per_expert
    gates = gates_ref[...]
    lane = lax.broadcasted_iota(jnp.int32, gates.shape, 1)
    gate = jnp.sum(jnp.where(lane == expert, gates, 0.0), axis=1, keepdims=True)
    o_ref[...] = (gate * (jax.nn.silu(ga) * gb)).astype(o_ref.dtype)


def matmul_moe_swiglu(a, w13, gates, *, bm=MM_BM, bn=MM_BN, cast=None):
    m, k = a.shape
    n_exp, f = w13.shape[0], w13.shape[-1] // 2
    bm = min(bm, m)
    assert m % bm == 0 and f % bn == 0
    bpe = f // bn
    blocks = (_nbytes((bm, k), a.dtype) + 2 * _nbytes((k, bn), w13.dtype) + _nbytes((bm, bn), BF16)
              + _nbytes((bm, V7X_LANES), F32))
    return _launch_matmul(
        functools.partial(_mm_moe_swiglu_body, blocks_per_expert=bpe), "matmul_moe_swiglu",
        (m // bm, n_exp * bpe),
        [pl.BlockSpec((bm, k), lambda i, j: (i, 0)),
         pl.BlockSpec((None, k, bn), lambda i, j: (j // bpe, 0, j % bpe)),
         pl.BlockSpec((None, k, bn), lambda i, j: (j // bpe, 0, j % bpe + bpe)),
         pl.BlockSpec((bm, V7X_LANES), lambda i, j: (i, 0))], (a, w13, w13, gates),
        (bm, bn), jax.ShapeDtypeStruct((m, n_exp * f), BF16), blocks, 4 * _nbytes((bm, bn), F32), cast)


def _mm_ple_body(a_ref, wg_ref, p_ref, wp_ref, *refs):
    *res_refs, o_ref = refs
    gate = jnp.dot(a_ref[...], wg_ref[...], preferred_element_type=F32)
    proj = jnp.dot(p_ref[...].astype(BF16), wp_ref[...], preferred_element_type=F32)
    o_ref[...] = DEEPNORM_ALPHA * _residual_tile(res_refs) + jax.nn.sigmoid(gate) * proj


def matmul_ple_residual(a, w_gate, p, w_proj, res, *, layer, bm=MM_BM, bn=MM_BN, cast=None):
    m, k = a.shape
    n = w_gate.shape[-1]
    kp = p.shape[-1]
    bm = min(bm, m)
    assert m % bm == 0 and n % bn == 0
    res_specs, res_args, res_bytes = _residual_operands(res, bm, bn)
    blocks = (_nbytes((bm, k), a.dtype) + _nbytes((k, bn), w_gate.dtype) + _nbytes((bm, kp), p.dtype)
              + _nbytes((kp, bn), w_proj.dtype) + _nbytes((bm, bn), F32) + res_bytes)
    return _launch_matmul(
        _mm_ple_body, "matmul_ple_residual", (m // bm, n // bn),
        [pl.BlockSpec((bm, k), lambda i, j: (i, 0)),
         _w_spec(w_gate, layer, k, bn),
         pl.BlockSpec((None, bm, kp), lambda i, j: (layer, i, 0)),
         _w_spec(w_proj, layer, kp, bn)] + res_specs, (a, w_gate, p, w_proj) + res_args,
        (bm, bn), jax.ShapeDtypeStruct((m, n), F32), blocks, 4 * _nbytes((bm, bn), F32), cast)


def _ln_normalise(y_ref, g_ref, b_ref):
    y = y_ref[...]
    mean = jnp.mean(y, axis=-1, keepdims=True)
    d = y - mean
    rstd = lax.rsqrt(jnp.mean(d * d, axis=-1, keepdims=True) + LN_EPS)
    return d * rstd * g_ref[...] + b_ref[...], mean, rstd


def _ln_stats_body(y_ref, g_ref, b_ref, ob_ref, mean_ref, rstd_ref):
    out, mean, rstd = _ln_normalise(y_ref, g_ref, b_ref)
    ob_ref[...] = out.astype(BF16)
    mean_ref[...] = jnp.broadcast_to(mean, mean_ref.shape)
    rstd_ref[...] = jnp.broadcast_to(rstd, rstd_ref.shape)


def _ln_f32_body(y_ref, g_ref, b_ref, o_ref):
    o_ref[...] = _ln_normalise(y_ref, g_ref, b_ref)[0]


def layer_norm(y, ln_g, ln_b, *, row, final=False, rows=LN_ROWS):
    m, d = y.shape
    rows = min(rows, m)
    assert m % rows == 0
    tile = pl.BlockSpec((rows, d), lambda i: (i, 0))
    stat = pl.BlockSpec((rows, V7X_LANES), lambda i: (i, 0))
    vec = pl.BlockSpec((None, 1, d), lambda i: (row, 0, 0))
    blocks = _nbytes((rows, d), F32) + 2 * _nbytes((8, d), F32)
    if final:
        return pl.pallas_call(
            _ln_f32_body, grid=(m // rows,), in_specs=[tile, vec, vec], out_specs=tile,
            out_shape=jax.ShapeDtypeStruct((m, d), F32),
            compiler_params=_compiler_params(("parallel",), blocks + _nbytes((rows, d), F32),
                                             3 * _nbytes((rows, d), F32)),
            name="layer_norm_f32",
        )(y, ln_g, ln_b)
    xb, mean, rstd = pl.pallas_call(
        _ln_stats_body, grid=(m // rows,), in_specs=[tile, vec, vec], out_specs=[tile, stat, stat],
        out_shape=[jax.ShapeDtypeStruct((m, d), BF16), jax.ShapeDtypeStruct((m, V7X_LANES), F32),
                   jax.ShapeDtypeStruct((m, V7X_LANES), F32)],
        compiler_params=_compiler_params(("parallel",), blocks + _nbytes((rows, d), BF16)
                                         + 2 * _nbytes((rows, V7X_LANES), F32), 3 * _nbytes((rows, d), F32)),
        name="layer_norm",
    )(y, ln_g, ln_b)
    return Normed(y, mean, rstd, ln_g, ln_b, row), xb


def _retention_body(q_ref, k_ref, v_ref, g_ref, dmask_ref, qdec_ref, kdec_ref, cdec_ref, o_ref, state_ref,
                    *, n_chunks, chunk):
    @pl.when(pl.program_id(2) == 0)
    def _():
        state_ref[...] = jnp.zeros_like(state_ref)

    dmask = dmask_ref[...]
    qdec = qdec_ref[...]
    kdec = kdec_ref[...]
    cdec = cdec_ref[...]
    for c in range(n_chunks):
        rows = pl.ds(c * chunk, chunk)
        q = q_ref[rows, :]
        k = k_ref[rows, :]
        v = v_ref[rows, :]
        state = state_ref[...]
        scores = lax.dot_general(q, k, (((1,), (1,)), ((), ())), preferred_element_type=F32) * dmask
        inner = jnp.dot(scores.astype(BF16), v, preferred_element_type=F32)
        cross = jnp.dot((q.astype(F32) * qdec).astype(BF16), state.astype(BF16), preferred_element_type=F32)
        k_scaled = (k.astype(F32) * kdec).astype(BF16)
        update = lax.dot_general(k_scaled, v, (((0,), (0,)), ((), ())), preferred_element_type=F32)
        state_ref[...] = state * cdec + update
        out = inner + cross
        mu = jnp.mean(out, axis=-1, keepdims=True)
        d = out - mu
        var = jnp.mean(d * d, axis=-1, keepdims=True)
        normed = d * lax.rsqrt(var + LN_EPS)
        gate = g_ref[rows, :].astype(F32)
        o_ref[rows, :] = (jax.nn.silu(gate) * normed).astype(o_ref.dtype)


def retention_tables():
    h, c = RET_HEADS, RET_CHUNK
    log_gamma = jnp.log1p(-jnp.exp2(-5.0 - jnp.arange(h, dtype=F32)))
    pos = jnp.arange(c, dtype=F32)
    rel = pos[:, None] - pos[None, :]
    dmask = jnp.where(rel >= 0, jnp.exp(log_gamma[:, None, None] * jnp.maximum(rel, 0.0)), 0.0)
    qdec = jnp.exp(log_gamma[:, None] * (pos + 1.0))[:, :, None]
    kdec = jnp.exp(log_gamma[:, None] * (c - 1.0 - pos))[:, :, None]
    cdec = jnp.exp(log_gamma * c)[:, None, None]
    return (dmask,
            jnp.broadcast_to(qdec, (h, c, RET_QK_DIM)),
            jnp.broadcast_to(kdec, (h, c, RET_QK_DIM)),
            jnp.broadcast_to(cdec, (h, 1, RET_V_DIM)))


def retention(qk, vg, tables, *, batch, seq, rows=RET_ROWS):
    h, dk, dv, c = RET_HEADS, RET_QK_DIM, RET_V_DIM, RET_CHUNK
    rows = min(rows, seq)
    assert seq % rows == 0 and rows % c == 0
    steps = seq // rows
    dmask, qdec, kdec, cdec = tables
    row_tile = lambda b, hh, l: b * steps + l
    blocks = (2 * _nbytes((rows, dk), BF16) + 3 * _nbytes((rows, dv), BF16) + _nbytes((c, c), F32)
              + 2 * _nbytes((c, dk), F32) + _nbytes((8, dv), F32))
    return pl.pallas_call(
        functools.partial(_retention_body, n_chunks=rows // c, chunk=c),
        grid=(batch, h, steps),
        in_specs=[pl.BlockSpec((rows, dk), lambda b, hh, l: (row_tile(b, hh, l), hh)),
                  pl.BlockSpec((rows, dk), lambda b, hh, l: (row_tile(b, hh, l), h + hh)),
                  pl.BlockSpec((rows, dv), lambda b, hh, l: (row_tile(b, hh, l), hh)),
                  pl.BlockSpec((rows, dv), lambda b, hh, l: (row_tile(b, hh, l), h + hh)),
                  pl.BlockSpec((None, c, c), lambda b, hh, l: (hh, 0, 0)),
                  pl.BlockSpec((None, c, dk), lambda b, hh, l: (hh, 0, 0)),
                  pl.BlockSpec((None, c, dk), lambda b, hh, l: (hh, 0, 0)),
                  pl.BlockSpec((None, 1, dv), lambda b, hh, l: (hh, 0, 0))],
        out_specs=pl.BlockSpec((rows, dv), lambda b, hh, l: (row_tile(b, hh, l), hh)),
        out_shape=jax.ShapeDtypeStruct((batch * seq, h * dv), BF16),
        scratch_shapes=[pltpu.VMEM((dk, dv), F32)],
        compiler_params=_compiler_params(("parallel", "parallel", "arbitrary"), blocks,
                                         _nbytes((dk, dv), F32) * 4 + _nbytes((c, dv), F32) * 8),
        name="retention",
    )(qk, qk, vg, vg, dmask, qdec, kdec, cdec)


def _diff_attention_body(q_ref, k_ref, v_ref, lam_ref, g_ref, o_ref, max_ref, sum_ref, acc_ref, s_ref, p_ref,
                         *, tile, strip, lambda_init):
    qi = pl.program_id(2)
    d = DIFF_HEAD_DIM
    lanes = V7X_LANES
    contract_last = (((1,), (1,)), ((), ()))

    def scores(kt, width):
        rows_k = pl.ds(pl.multiple_of(kt * tile, tile), width)
        for c in range(2):
            s_ref[c, :, :width] = lax.dot_general(q_ref[:, c * d:(c + 1) * d], k_ref[rows_k, c * d:(c + 1) * d],
                                                  contract_last, preferred_element_type=F32)
        return rows_k

    def block(c, r, cb, diagonal):
        if diagonal and cb * lanes >= (r + 1) * strip:
            return None
        s = s_ref[c, r * strip:(r + 1) * strip, cb * lanes:(cb + 1) * lanes]
        if diagonal and (cb + 1) * lanes - 1 > r * strip:
            row_id = r * strip + lax.broadcasted_iota(jnp.int32, (strip, lanes), 0)
            col_id = cb * lanes + lax.broadcasted_iota(jnp.int32, (strip, lanes), 1)
            s = jnp.where(row_id >= col_id, s, -jnp.inf)
        return s

    def max_sweep(kt, width, diagonal):
        scores(kt, width)
        for c in range(2):
            for r in range(tile // strip):
                rows = slice(r * strip, (r + 1) * strip)
                blocks = [block(c, r, cb, diagonal) for cb in range(width // lanes)]
                max_ref[c, rows] = functools.reduce(jnp.maximum,
                                                    [max_ref[c, rows]] + [b for b in blocks if b is not None])

    def sum_sweep(kt, width, diagonal):
        rows_k = scores(kt, width)
        for c in range(2):
            for r in range(tile // strip):
                rows = slice(r * strip, (r + 1) * strip)
                row_max = max_ref[c, rows]
                total = sum_ref[c, rows]
                for cb in range(width // lanes):
                    s = block(c, r, cb, diagonal)
                    if s is None:
                        p_ref[c, rows, cb * lanes:(cb + 1) * lanes] = jnp.zeros((strip, lanes), BF16)
                        continue
                    p = jnp.exp2(s - row_max)
                    total = total + p
                    p_ref[c, rows, cb * lanes:(cb + 1) * lanes] = p.astype(BF16)
                sum_ref[c, rows] = total
        v = v_ref[rows_k, :]
        for c in range(2):
            acc_ref[c] += jnp.dot(p_ref[c, :, :width], v, preferred_element_type=F32)

    def loop(sweep):
        def pair(n, carry):
            sweep(2 * n, 2 * tile, False)
            return carry
        lax.fori_loop(0, qi // 2, pair, 0)

        @pl.when(qi % 2 == 1)
        def _():
            sweep(qi - 1, tile, False)
        sweep(qi, tile, True)

    max_ref[...] = jnp.full_like(max_ref, -jnp.inf)
    sum_ref[...] = jnp.zeros_like(sum_ref)
    acc_ref[...] = jnp.zeros_like(acc_ref)
    loop(max_sweep)
    for c in range(2):
        max_ref[c] = jnp.broadcast_to(jnp.max(max_ref[c], axis=1, keepdims=True), (tile, lanes))
    loop(sum_sweep)

    lam = lam_ref[...]
    lam_full = (jnp.exp(jnp.sum(lam[0:1] * lam[1:2], axis=1, keepdims=True))
                - jnp.exp(jnp.sum(lam[2:3] * lam[3:4], axis=1, keepdims=True)) + lambda_init)
    inv_l = [1.0 / jnp.sum(sum_ref[c], axis=1, keepdims=True) for c in range(2)]
    out = acc_ref[0] * inv_l[0] - lam_full * (acc_ref[1] * inv_l[1])
    y = out * lax.rsqrt(jnp.mean(out * out, axis=-1, keepdims=True) + LN_EPS)
    o_ref[...] = ((y * g_ref[...]) * (1.0 - lambda_init)).astype(o_ref.dtype)


def diff_attention(q, k, v, lam, subln_g, *, layer, batch, seq, lambda_init, tile=ATT_TILE, strip=ATT_STRIP):
    h, dv = DIFF_HEADS, DIFF_V_DIM
    tile = min(tile, seq)
    assert seq % tile == 0 and tile % strip == 0
    nq = seq // tile
    blocks = (2 * _nbytes((tile, dv), BF16) + 2 * _nbytes((seq, dv), BF16) + _nbytes((8, V7X_LANES), F32)
              + _nbytes((8, dv), F32))
    scratch = [pltpu.VMEM((2, tile, V7X_LANES), F32), pltpu.VMEM((2, tile, V7X_LANES), F32),
               pltpu.VMEM((2, tile, dv), F32), pltpu.VMEM((2, tile, 2 * tile), F32),
               pltpu.VMEM((2, tile, 2 * tile), BF16)]
    scratch_bytes = (2 * 2 * _nbytes((tile, V7X_LANES), F32) + 2 * _nbytes((tile, dv), F32)
                     + 2 * _nbytes((tile, 2 * tile), F32) + 2 * _nbytes((tile, 2 * tile), BF16))
    return pl.pallas_call(
        functools.partial(_diff_attention_body, tile=tile, strip=strip, lambda_init=lambda_init),
        grid=(batch, h, nq),
        in_specs=[pl.BlockSpec((tile, dv), lambda b, hh, qi: (b * nq + qi, hh)),
                  pl.BlockSpec((seq, dv), lambda b, hh, qi: (b, hh)),
                  pl.BlockSpec((seq, dv), lambda b, hh, qi: (b, hh)),
                  pl.BlockSpec((None, 4, DIFF_HEAD_DIM), lambda b, hh, qi: (layer, 0, 0)),
                  pl.BlockSpec((None, 1, dv), lambda b, hh, qi: (layer, 0, 0))],
        out_specs=pl.BlockSpec((tile, dv), lambda b, hh, qi: (b * nq + qi, hh)),
        out_shape=jax.ShapeDtypeStruct((batch * seq, h * dv), BF16),
        scratch_shapes=scratch,
        compiler_params=_compiler_params(("parallel", "parallel", "arbitrary"), blocks,
                                         scratch_bytes + 2 * _nbytes((tile, dv), F32)),
        name="diff_attention",
    )(q, k, v, lam, subln_g)


def _router_body(a_ref, w_ref, o_ref):
    logits = jnp.dot(a_ref[...], w_ref[...], preferred_element_type=F32)
    lane = lax.broadcasted_iota(jnp.int32, logits.shape, 1).astype(F32)
    logits = jnp.where(lane < N_EXPERTS, logits, -jnp.inf)
    top1 = jnp.max(logits, axis=1, keepdims=True)
    idx1 = jnp.min(jnp.where(logits == top1, lane, float(V7X_LANES)), axis=1, keepdims=True)
    rest = jnp.where(lane == idx1, -jnp.inf, logits)
    top2 = jnp.max(rest, axis=1, keepdims=True)
    idx2 = jnp.min(jnp.where(rest == top2, lane, float(V7X_LANES)), axis=1, keepdims=True)
    e = jnp.exp(top2 - top1)
    w1 = 1.0 / (1.0 + e)
    w2 = e / (1.0 + e)
    o_ref[...] = jnp.where(lane == idx1, w1, 0.0) + jnp.where(lane == idx2, w2, 0.0)


def router_gates(a, w_router_padded, *, layer, bm=MM_BM):
    m, k = a.shape
    bm = min(bm, m)
    assert m % bm == 0
    blocks = _nbytes((bm, k), a.dtype) + _nbytes((k, V7X_LANES), BF16) + _nbytes((bm, V7X_LANES), F32)
    return pl.pallas_call(
        _router_body,
        grid=(m // bm,),
        in_specs=[pl.BlockSpec((bm, k), lambda i: (i, 0)),
                  pl.BlockSpec((None, k, V7X_LANES), lambda i: (layer, 0, 0))],
        out_specs=pl.BlockSpec((bm, V7X_LANES), lambda i: (i, 0)),
        out_shape=jax.ShapeDtypeStruct((m, V7X_LANES), F32),
        compiler_params=_compiler_params(("parallel",), blocks, 8 * _nbytes((bm, V7X_LANES), F32)),
        name="router_gates",
    )(a, w_router_padded)


def _rope_tables(seq, dim):
    inv_freq = 1.0 / (ROPE_THETA ** (jnp.arange(0, dim, 2, dtype=F32) / dim))
    ang = jnp.arange(seq, dtype=F32)[:, None] * inv_freq[None, :]
    cos, sin = jnp.cos(ang), jnp.sin(ang)
    return jnp.concatenate([cos, cos], axis=-1), jnp.concatenate([-sin, sin], axis=-1)


def kernel(x, p, ret_w_in, ret_w_out, kv_w, diff_w_q, diff_w_out, diff_lambda, diff_subln_g, ffn_w13, ffn_w2,
           moe_router, moe_w13, moe_w2, ple_w_gate, ple_w_proj, ln_g, ln_b):
    batch, seq, d = x.shape
    assert d == D_MODEL
    tokens = batch * seq
    cos_r, sin_r = _rope_tables(seq, RET_QK_DIM)
    cos_d, sin_d = _rope_tables(seq, DIFF_HEAD_DIM)
    ret_tabs = retention_tables()

    w_in0 = ret_w_in[0].astype(BF16)
    ple_w_proj = ple_w_proj.astype(BF16)
    router_w = jnp.pad(moe_router.astype(BF16), ((0, 0), (0, 0), (0, V7X_LANES - N_EXPERTS)))
    flat = lambda w: w.reshape(-1, w.shape[-1])
    n_ret, n_dense, n_moe = ret_w_out.shape[0], ffn_w13.shape[0], moe_w13.shape[0]
    ln_g = ln_g.reshape(DEPTH * 3, 1, D_MODEL)
    ln_b = ln_b.reshape(DEPTH * 3, 1, D_MODEL)
    subln_g = diff_subln_g.reshape(-1, 1, DIFF_V_DIM)
    p = p.reshape(DEPTH, tokens, PLE_DIM)

    x = x.reshape(tokens, D_MODEL)
    xb = x.astype(BF16)
    qk_w = RET_HEADS * RET_QK_DIM
    v_w = RET_HEADS * RET_V_DIM
    kd = 2 * DIFF_HEADS * DIFF_HEAD_DIM
    vd = DIFF_HEADS * DIFF_V_DIM
    q_scale = DIFF_HEAD_DIM ** -0.5 * LOG2_E

    def retention_layer(i, w_in, w_out, x, xb, vg_cast, out_cast):
        qk = matmul_rope(xb, w_in, cos_r, sin_r, col_off=0, n_out=2 * qk_w, head_dim=RET_QK_DIM,
                         split_col=qk_w, scale_lo=1.0, scale_hi=RET_QK_DIM ** -0.5)
        vg, cast_a = matmul(xb, w_in, col_off=2 * qk_w, n_out=2 * v_w, out_dtype=BF16, cast=vg_cast)
        w_out = w_out if w_out is not None else cast_a.reshape(ret_w_out.shape)
        gated = retention(qk, vg, ret_tabs, batch=batch, seq=seq)
        y, cast_b = matmul_residual(gated, w_out, x, layer=i, cast=out_cast)
        return y, w_out, cast_a, cast_b

    def attention_layer(j, w_q, w_out, k_sh, v_sh, x, xb, out_cast):
        lambda_init = 0.8 - 0.6 * math.exp(-0.3 * (j + N_A_LAYERS))
        q = matmul_rope(xb, w_q, cos_d, sin_d, layer=j, n_out=kd, head_dim=DIFF_HEAD_DIM,
                        scale_lo=q_scale, scale_hi=q_scale)
        att = diff_attention(q, k_sh, v_sh, diff_lambda, subln_g, layer=j, batch=batch, seq=seq,
                             lambda_init=lambda_init)
        return matmul_residual(att, w_out, x, layer=j, cast=out_cast)

    y, w_ret_out, _, w13_d0 = retention_layer(0, w_in0, None, x, xb, (flat(ret_w_out), 0, 1),
                                               (flat(ffn_w13), 0, n_dense))
    x, xb = layer_norm(y, ln_g, ln_b, row=0)
    hidden, w2_d = matmul_swiglu(xb, w13_d0, cast=(flat(ffn_w2), 0, 1))
    w2_d = w2_d.reshape(ffn_w2.shape)
    y, w_ple = matmul_residual(hidden, w2_d, x, layer=0, cast=(flat(ple_w_gate), 0, 1))
    w_ple = w_ple.reshape(ple_w_gate.shape)
    x, xb = layer_norm(y, ln_g, ln_b, row=1)
    y, w_in1 = matmul_ple_residual(xb, w_ple, p, ple_w_proj, x, layer=0, cast=(flat(ret_w_in), 1, n_ret))
    x, xb = layer_norm(y, ln_g, ln_b, row=2)

    y, _, w13_m0, w2_m = retention_layer(1, w_in1, w_ret_out, x, xb, (flat(moe_w13), 0, n_moe),
                                         (flat(moe_w2), 0, 1))
    w13_m0 = w13_m0.reshape(moe_w13.shape[1:])
    w2_m = w2_m.reshape(n_moe, N_EXPERTS * D_EXPERT, D_MODEL)
    x, xb = layer_norm(y, ln_g, ln_b, row=3)
    gates = router_gates(xb, router_w, layer=0)
    hidden, w_kv = matmul_moe_swiglu(xb, w13_m0, gates, cast=(kv_w, 0, 1))
    y, w_q = matmul_residual(hidden, w2_m, x, layer=0, cast=(flat(diff_w_q), 0, 1))
    w_q = w_q.reshape(diff_w_q.shape)
    x, xb = layer_norm(y, ln_g, ln_b, row=4)
    y, w_att_out = matmul_ple_residual(xb, w_ple, p, ple_w_proj, x, layer=1, cast=(flat(diff_w_out), 0, 1))
    w_att_out = w_att_out.reshape(diff_w_out.shape)
    x, xb = layer_norm(y, ln_g, ln_b, row=5)
    k_sh = matmul_rope(xb, w_kv, cos_d, sin_d, col_off=0, n_out=kd, head_dim=DIFF_HEAD_DIM)
    v_sh = matmul(xb, w_kv, col_off=kd, n_out=vd, out_dtype=BF16)

    y, w13_d1 = attention_layer(0, w_q, w_att_out, k_sh, v_sh, x, xb, (flat(ffn_w13), 1, n_dense))
    x, xb = layer_norm(y, ln_g, ln_b, row=6)
    hidden, w13_m1 = matmul_swiglu(xb, w13_d1, cast=(flat(moe_w13), 1, n_moe))
    w13_m1 = w13_m1.reshape(moe_w13.shape[1:])
    y = matmul_residual(hidden, w2_d, x, layer=1)
    x, xb = layer_norm(y, ln_g, ln_b, row=7)
    y = matmul_ple_residual(xb, w_ple, p, ple_w_proj, x, layer=2)
    x, xb = layer_norm(y, ln_g, ln_b, row=8)

    y = attention_layer(1, w_q, w_att_out, k_sh, v_sh, x, xb, None)
    x, xb = layer_norm(y, ln_g, ln_b, row=9)
    gates = router_gates(xb, router_w, layer=1)
    hidden = matmul_moe_swiglu(xb, w13_m1, gates)
    y = matmul_residual(hidden, w2_m, x, layer=1)
    x, xb = layer_norm(y, ln_g, ln_b, row=10)
    y = matmul_ple_residual(xb, w_ple, p, ple_w_proj, x, layer=3)
    out = layer_norm(y, ln_g, ln_b, row=11, final=True)
    return out.reshape(batch, seq, D_MODEL)
```

```python
import functools
import math
from typing import NamedTuple

import jax
import jax.numpy as jnp
from jax import lax
from jax.experimental import pallas as pl
from jax.experimental.pallas import tpu as pltpu

F32 = jnp.float32
BF16 = jnp.bfloat16

D_MODEL = 4096
DEPTH = 4
N_A_LAYERS = DEPTH // 2
RET_HEADS = 16
RET_QK_DIM = D_MODEL // RET_HEADS
RET_V_DIM = 2 * D_MODEL // RET_HEADS
RET_CHUNK = 128
DIFF_HEADS = 16
DIFF_HEAD_DIM = D_MODEL // (2 * DIFF_HEADS)
DIFF_V_DIM = 2 * DIFF_HEAD_DIM
ROPE_THETA = 10000.0
D_FF = 2 * D_MODEL
N_EXPERTS = 8
TOP_K = 2
D_EXPERT = D_MODEL // 4
PLE_DIM = 256
LN_EPS = 1e-5
DEEPNORM_ALPHA = (2.0 * DEPTH) ** 0.25

V7X_LANES = 128
V7X_VMEM_BYTES = 64 * 1024 * 1024
V7X_VMEM_CAP = V7X_VMEM_BYTES - 6 * 1024 * 1024

MM_BM = 1024
MM_BN_WIDE_K = 256
MM_BN = 512
MM_BN_WIDE = 1024
LN_ROWS = 256
RET_ROWS = 1024
ATT_TILE = 1024
ATT_STRIP = 64
LOG2_E = math.log2(math.e)


def _nbytes(shape, dtype):
    return math.prod(shape) * jnp.dtype(dtype).itemsize


def _compiler_params(semantics, block_bytes, temp_bytes):
    need = 2 * block_bytes + temp_bytes + (2 << 20)
    return pltpu.CompilerParams(
        dimension_semantics=semantics,
        vmem_limit_bytes=int(min(max(need, 16 << 20), V7X_VMEM_CAP)),
    )


def _w_spec(w, layer, k, bn, col_block_off=0):
    if w.ndim == 3:
        return pl.BlockSpec((None, k, bn), lambda i, j: (layer, 0, j + col_block_off))
    return pl.BlockSpec((k, bn), lambda i, j: (0, j + col_block_off))


def _hosting_cast(body, n_in):
    def hosted(*refs):
        body(*refs[:n_in], refs[n_in + 1])
        refs[n_in + 2][...] = refs[n_in][...].astype(BF16)
    return hosted


def _launch_matmul(body, name, grid, in_specs, args, out_block, out_shape, block_bytes, temp_bytes, cast=None):
    out_spec = pl.BlockSpec(out_block, lambda i, j: (i, j))
    semantics = ("parallel", "arbitrary")
    if cast is None:
        return pl.pallas_call(
            body, grid=grid, in_specs=in_specs, out_specs=out_spec, out_shape=out_shape,
            compiler_params=_compiler_params(semantics, block_bytes, temp_bytes), name=name,
        )(*args)
    src, part, n_parts = cast
    rows, cols = src.shape[0] // n_parts, src.shape[1]
    steps = grid[0] * grid[1]
    slab = rows // steps
    assert src.shape[0] % n_parts == 0 and rows % steps == 0 and slab % 16 == 0
    block_bytes += _nbytes((slab, cols), F32) + _nbytes((slab, cols), BF16)
    return pl.pallas_call(
        _hosting_cast(body, len(in_specs)),
        grid=grid,
        in_specs=list(in_specs) + [pl.BlockSpec((slab, cols), lambda i, j: (part * steps + i * grid[1] + j, 0))],
        out_specs=[out_spec, pl.BlockSpec((slab, cols), lambda i, j: (i * grid[1] + j, 0))],
        out_shape=[out_shape, jax.ShapeDtypeStruct((rows, cols), BF16)],
        compiler_params=_compiler_params(semantics, block_bytes, temp_bytes), name=name + "_cast",
    )(*args, src)


def _mm_plain_body(a_ref, w_ref, o_ref):
    o_ref[...] = jnp.dot(a_ref[...], w_ref[...], preferred_element_type=F32).astype(o_ref.dtype)


def matmul(a, w, *, layer=None, col_off=0, n_out, out_dtype, cast=None):
    m, k = a.shape
    bm, bn = MM_BM, (MM_BN_WIDE if k <= D_MODEL else MM_BN_WIDE_K)
    bm = min(bm, m)
    assert m % bm == 0 and n_out % bn == 0 and col_off % bn == 0
    blocks = _nbytes((bm, k), a.dtype) + _nbytes((k, bn), w.dtype) + _nbytes((bm, bn), out_dtype)
    return _launch_matmul(
        _mm_plain_body, "matmul", (m // bm, n_out // bn),
        [pl.BlockSpec((bm, k), lambda i, j: (i, 0)), _w_spec(w, layer, k, bn, col_off // bn)], (a, w),
        (bm, bn), jax.ShapeDtypeStruct((m, n_out), out_dtype), blocks, 2 * _nbytes((bm, bn), F32), cast)


class Normed(NamedTuple):
    y: jax.Array
    mean: jax.Array
    rstd: jax.Array
    gain: jax.Array
    bias: jax.Array
    row: int


def _residual_operands(res, bm, bn):
    tile = pl.BlockSpec((bm, bn), lambda i, j: (i, j))
    if not isinstance(res, Normed):
        return [tile], (res,), _nbytes((bm, bn), F32)
    stat = pl.BlockSpec((bm, V7X_LANES), lambda i, j: (i, 0))
    vec = pl.BlockSpec((None, 1, bn), lambda i, j: (res.row, 0, j))
    nbytes = _nbytes((bm, bn), F32) + 2 * _nbytes((bm, V7X_LANES), F32) + 2 * _nbytes((8, bn), F32)
    return [tile, stat, stat, vec, vec], (res.y, res.mean, res.rstd, res.gain, res.bias), nbytes


def _residual_tile(refs):
    if len(refs) == 1:
        return refs[0][...]
    y_ref, mean_ref, rstd_ref, gain_ref, bias_ref = refs
    reps = y_ref.shape[1] // V7X_LANES
    mean = jnp.concatenate([mean_ref[...]] * reps, axis=1)
    rstd = jnp.concatenate([rstd_ref[...]] * reps, axis=1)
    return (y_ref[...] - mean) * rstd * gain_ref[...] + bias_ref[...]


def _mm_residual_body(a_ref, w_ref, *refs):
    *res_refs, o_ref = refs
    o_ref[...] = (DEEPNORM_ALPHA * _residual_tile(res_refs)
                  + jnp.dot(a_ref[...], w_ref[...], preferred_element_type=F32))


def matmul_residual(a, w, res, *, layer, cast=None):
    m, k = a.shape
    n = w.shape[-1]
    bm, bn = MM_BM, (MM_BN if k <= D_MODEL else MM_BN_WIDE_K)
    bm = min(bm, m)
    assert m % bm == 0 and n % bn == 0
    res_specs, res_args, res_bytes = _residual_operands(res, bm, bn)
    blocks = _nbytes((bm, k), a.dtype) + _nbytes((k, bn), w.dtype) + _nbytes((bm, bn), F32) + res_bytes
    return _launch_matmul(
        _mm_residual_body, "matmul_residual", (m // bm, n // bn),
        [pl.BlockSpec((bm, k), lambda i, j: (i, 0)), _w_spec(w, layer, k, bn)] + res_specs, (a, w) + res_args,
        (bm, bn), jax.ShapeDtypeStruct((m, n), F32), blocks, 3 * _nbytes((bm, bn), F32), cast)


def _mm_rope_body(a_ref, w_ref, cos_ref, sin_ref, o_ref, *, head_dim, split_block, scale_lo, scale_hi):
    acc = jnp.dot(a_ref[...], w_ref[...], preferred_element_type=F32)
    cos = cos_ref[...]
    sin = sin_ref[...]
    if scale_lo == scale_hi:
        scale = scale_lo
    else:
        scale = jnp.where(pl.program_id(1) < split_block, scale_lo, scale_hi)
    for h in range(acc.shape[1] // head_dim):
        cols = slice(h * head_dim, (h + 1) * head_dim)
        xb = acc[:, cols]
        rot = pltpu.roll(xb, head_dim // 2, axis=1)
        o_ref[:, cols] = ((xb * cos + rot * sin) * scale).astype(o_ref.dtype)


def matmul_rope(a, w, cos_full, sin_signed, *, layer=None, col_off=0, n_out, head_dim,
                split_col=0, scale_lo=1.0, scale_hi=1.0, bm=MM_BM, bn=MM_BN_WIDE):
    m, k = a.shape
    seq = cos_full.shape[0]
    bm = min(bm, seq)
    assert m % bm == 0 and seq % bm == 0 and n_out % bn == 0 and bn % head_dim == 0
    assert col_off % bn == 0 and split_col % bn == 0
    pos_tiles = seq // bm
    blocks = (_nbytes((bm, k), a.dtype) + _nbytes((k, bn), w.dtype) + _nbytes((bm, bn), BF16)
              + 2 * _nbytes((bm, head_dim), F32))
    body = functools.partial(_mm_rope_body, head_dim=head_dim, split_block=split_col // bn,
                             scale_lo=scale_lo, scale_hi=scale_hi)
    return _launch_matmul(
        body, "matmul_rope", (m // bm, n_out // bn),
        [pl.BlockSpec((bm, k), lambda i, j: (i, 0)),
         _w_spec(w, layer, k, bn, col_off // bn),
         pl.BlockSpec((bm, head_dim), lambda i, j: (i % pos_tiles, 0)),
         pl.BlockSpec((bm, head_dim), lambda i, j: (i % pos_tiles, 0))], (a, w, cos_full, sin_signed),
        (bm, bn), jax.ShapeDtypeStruct((m, n_out), BF16), blocks, 3 * _nbytes((bm, bn), F32))


def _mm_swiglu_body(a_ref, wa_ref, wb_ref, o_ref):
    a = a_ref[...]
    ga = jnp.dot(a, wa_ref[...], preferred_element_type=F32)
    gb = jnp.dot(a, wb_ref[...], preferred_element_type=F32)
    o_ref[...] = (jax.nn.silu(ga) * gb).astype(o_ref.dtype)


def matmul_swiglu(a, w13, *, layer=None, bm=MM_BM, bn=MM_BN, cast=None):
    m, k = a.shape
    f = w13.shape[-1] // 2
    bm = min(bm, m)
    assert m % bm == 0 and f % bn == 0
    blocks = _nbytes((bm, k), a.dtype) + 2 * _nbytes((k, bn), w13.dtype) + _nbytes((bm, bn), BF16)
    return _launch_matmul(
        _mm_swiglu_body, "matmul_swiglu", (m // bm, f // bn),
        [pl.BlockSpec((bm, k), lambda i, j: (i, 0)), _w_spec(w13, layer, k, bn, 0),
         _w_spec(w13, layer, k, bn, f // bn)], (a, w13, w13),
        (bm, bn), jax.ShapeDtypeStruct((m, f), BF16), blocks, 4 * _nbytes((bm, bn), F32), cast)


def _mm_moe_swiglu_body(a_ref, wa_ref, wb_ref, gates_ref, o_ref, *, blocks_per_expert):
    a = a_ref[...]
    ga = jnp.dot(a, wa_ref[...], preferred_element_type=F32)
    gb = jnp.dot(a, wb_ref[...], preferred_element_type=F32)
    expert = pl.program_id(1) // blocks_per_expert
    gates = gates_ref[...]
    lane = lax.broadcasted_iota(jnp.int32, gates.shape, 1)
    gate = jnp.sum(jnp.where(lane == expert, gates, 0.0), axis=1, keepdims=True)
    o_ref[...] = (gate * (jax.nn.silu(ga) * gb)).astype(o_ref.dtype)


def matmul_moe_swiglu(a, w13, gates, *, bm=MM_BM, bn=MM_BN, cast=None):
    m, k = a.shape
    n_exp, f = w13.shape[0], w13.shape[-1] // 2
    bm = min(bm, m)
    assert m % bm == 0 and f % bn == 0
    bpe = f // bn
    blocks = (_nbytes((bm, k), a.dtype) + 2 * _nbytes((k, bn), w13.dtype) + _nbytes((bm, bn), BF16)
              + _nbytes((bm, V7X_LANES), F32))
    return _launch_matmul(
        functools.partial(_mm_moe_swiglu_body, blocks_per_expert=bpe), "matmul_moe_swiglu",
        (m // bm, n_exp * bpe),
        [pl.BlockSpec((bm, k), lambda i, j: (i, 0)),
         pl.BlockSpec((None, k, bn), lambda i, j: (j // bpe, 0, j % bpe)),
         pl.BlockSpec((None, k, bn), lambda i, j: (j // bpe, 0, j % bpe + bpe)),
         pl.BlockSpec((bm, V7X_LANES), lambda i, j: (i, 0))], (a, w13, w13, gates),
        (bm, bn), jax.ShapeDtypeStruct((m, n_exp * f), BF16), blocks, 4 * _nbytes((bm, bn), F32), cast)


def _mm_ple_body(a_ref, wg_ref, p_ref, wp_ref, *refs):
    *res_refs, o_ref = refs
    gate = jnp.dot(a_ref[...], wg_ref[...], preferred_element_type=F32)
    proj = jnp.dot(p_ref[...].astype(BF16), wp_ref[...], preferred_element_type=F32)
    o_ref[...] = DEEPNORM_ALPHA * _residual_tile(res_refs) + jax.nn.sigmoid(gate) * proj


def matmul_ple_residual(a, w_gate, p, w_proj, res, *, layer, bm=MM_BM, bn=MM_BN, cast=None):
    m, k = a.shape
    n = w_gate.shape[-1]
    kp = p.shape[-1]
    bm = min(bm, m)
    assert m % bm == 0 and n % bn == 0
    res_specs, res_args, res_bytes = _residual_operands(res, bm, bn)
    blocks = (_nbytes((bm, k), a.dtype) + _nbytes((k, bn), w_gate.dtype) + _nbytes((bm, kp), p.dtype)
              + _nbytes((kp, bn), w_proj.dtype) + _nbytes((bm, bn), F32) + res_bytes)
    return _launch_matmul(
        _mm_ple_body, "matmul_ple_residual", (m // bm, n // bn),
        [pl.BlockSpec((bm, k), lambda i, j: (i, 0)),
         _w_spec(w_gate, layer, k, bn),
         pl.BlockSpec((None, bm, kp), lambda i, j: (layer, i, 0)),
         _w_spec(w_proj, layer, kp, bn)] + res_specs, (a, w_gate, p, w_proj) + res_args,
        (bm, bn), jax.ShapeDtypeStruct((m, n), F32), blocks, 4 * _nbytes((bm, bn), F32), cast)


def _ln_normalise(y_ref, g_ref, b_ref):
    y = y_ref[...]
    mean = jnp.mean(y, axis=-1, keepdims=True)
    d = y - mean
    rstd = lax.rsqrt(jnp.mean(d * d, axis=-1, keepdims=True) + LN_EPS)
    return d * rstd * g_ref[...] + b_ref[...], mean, rstd


def _ln_stats_body(y_ref, g_ref, b_ref, ob_ref, mean_ref, rstd_ref):
    out, mean, rstd = _ln_normalise(y_ref, g_ref, b_ref)
    ob_ref[...] = out.astype(BF16)
    mean_ref[...] = jnp.broadcast_to(mean, mean_ref.shape)
    rstd_ref[...] = jnp.broadcast_to(rstd, rstd_ref.shape)


def _ln_f32_body(y_ref, g_ref, b_ref, o_ref):
    o_ref[...] = _ln_normalise(y_ref, g_ref, b_ref)[0]


def layer_norm(y, ln_g, ln_b, *, row, final=False, rows=LN_ROWS):
    m, d = y.shape
    rows = min(rows, m)
    assert m % rows == 0
    tile = pl.BlockSpec((rows, d), lambda i: (i, 0))
    stat = pl.BlockSpec((rows, V7X_LANES), lambda i: (i, 0))
    vec = pl.BlockSpec((None, 1, d), lambda i: (row, 0, 0))
    blocks = _nbytes((rows, d), F32) + 2 * _nbytes((8, d), F32)
    if final:
        return pl.pallas_call(
            _ln_f32_body, grid=(m // rows,), in_specs=[tile, vec, vec], out_specs=tile,
            out_shape=jax.ShapeDtypeStruct((m, d), F32),
            compiler_params=_compiler_params(("parallel",), blocks + _nbytes((rows, d), F32),
                                             3 * _nbytes((rows, d), F32)),
            name="layer_norm_f32",
        )(y, ln_g, ln_b)
    xb, mean, rstd = pl.pallas_call(
        _ln_stats_body, grid=(m // rows,), in_specs=[tile, vec, vec], out_specs=[tile, stat, stat],
        out_shape=[jax.ShapeDtypeStruct((m, d), BF16), jax.ShapeDtypeStruct((m, V7X_LANES), F32),
                   jax.ShapeDtypeStruct((m, V7X_LANES), F32)],
        compiler_params=_compiler_params(("parallel",), blocks + _nbytes((rows, d), BF16)
                                         + 2 * _nbytes((rows, V7X_LANES), F32), 3 * _nbytes((rows, d), F32)),
        name="layer_norm",
    )(y, ln_g, ln_b)
    return Normed(y, mean, rstd, ln_g, ln_b, row), xb


def _retention_body(q_ref, k_ref, v_ref, g_ref, dmask_ref, qdec_ref, kdec_ref, cdec_ref, o_ref, state_ref,
                    *, n_chunks, chunk):
    @pl.when(pl.program_id(2) == 0)
    def _():
        state_ref[...] = jnp.zeros_like(state_ref)

    dmask = dmask_ref[...]
    qdec = qdec_ref[...]
    kdec = kdec_ref[...]
    cdec = cdec_ref[...]
    for c in range(n_chunks):
        rows = pl.ds(c * chunk, chunk)
        q = q_ref[rows, :]
        k = k_ref[rows, :]
        v = v_ref[rows, :]
        state = state_ref[...]
        scores = lax.dot_general(q, k, (((1,), (1,)), ((), ())), preferred_element_type=F32) * dmask
        inner = jnp.dot(scores.astype(BF16), v, preferred_element_type=F32)
        cross = jnp.dot((q.astype(F32) * qdec).astype(BF16), state.astype(BF16), preferred_element_type=F32)
        k_scaled = (k.astype(F32) * kdec).astype(BF16)
        update = lax.dot_general(k_scaled, v, (((0,), (0,)), ((), ())), preferred_element_type=F32)
        state_ref[...] = state * cdec + update
        out = inner + cross
        mu = jnp.mean(out, axis=-1, keepdims=True)
        d = out - mu
        var = jnp.mean(d * d, axis=-1, keepdims=True)
        normed = d * lax.rsqrt(var + LN_EPS)
        gate = g_ref[rows, :].astype(F32)
        o_ref[rows, :] = (jax.nn.silu(gate) * normed).astype(o_ref.dtype)


def retention_tables():
    h, c = RET_HEADS, RET_CHUNK
    log_gamma = jnp.log1p(-jnp.exp2(-5.0 - jnp.arange(h, dtype=F32)))
    pos = jnp.arange(c, dtype=F32)
    rel = pos[:, None] - pos[None, :]
    dmask = jnp.where(rel >= 0, jnp.exp(log_gamma[:, None, None] * jnp.maximum(rel, 0.0)), 0.0)
    qdec = jnp.exp(log_gamma[:, None] * (pos + 1.0))[:, :, None]
    kdec = jnp.exp(log_gamma[:, None] * (c - 1.0 - pos))[:, :, None]
    cdec = jnp.exp(log_gamma * c)[:, None, None]
    return (dmask,
            jnp.broadcast_to(qdec, (h, c, RET_QK_DIM)),
            jnp.broadcast_to(kdec, (h, c, RET_QK_DIM)),
            jnp.broadcast_to(cdec, (h, 1, RET_V_DIM)))


def retention(qk, vg, tables, *, batch, seq, rows=RET_ROWS):
    h, dk, dv, c = RET_HEADS, RET_QK_DIM, RET_V_DIM, RET_CHUNK
    rows = min(rows, seq)
    assert seq % rows == 0 and rows % c == 0
    steps = seq // rows
    dmask, qdec, kdec, cdec = tables
    row_tile = lambda b, hh, l: b * steps + l
    blocks = (2 * _nbytes((rows, dk), BF16) + 3 * _nbytes((rows, dv), BF16) + _nbytes((c, c), F32)
              + 2 * _nbytes((c, dk), F32) + _nbytes((8, dv), F32))
    return pl.pallas_call(
        functools.partial(_retention_body, n_chunks=rows // c, chunk=c),
        grid=(batch, h, steps),
        in_specs=[pl.BlockSpec((rows, dk), lambda b, hh, l: (row_tile(b, hh, l), hh)),
                  pl.BlockSpec((rows, dk), lambda b, hh, l: (row_tile(b, hh, l), h + hh)),
                  pl.BlockSpec((rows, dv), lambda b, hh, l: (row_tile(b, hh, l), hh)),
                  pl.BlockSpec((rows, dv), lambda b, hh, l: (row_tile(b, hh, l), h + hh)),
                  pl.BlockSpec((None, c, c), lambda b, hh, l: (hh, 0, 0)),
                  pl.BlockSpec((None, c, dk), lambda b, hh, l: (hh, 0, 0)),
                  pl.BlockSpec((None, c, dk), lambda b, hh, l: (hh, 0, 0)),
                  pl.BlockSpec((None, 1, dv), lambda b, hh, l: (hh, 0, 0))],
        out_specs=pl.BlockSpec((rows, dv), lambda b, hh, l: (row_tile(b, hh, l), hh)),
        out_shape=jax.ShapeDtypeStruct((batch * seq, h * dv), BF16),
        scratch_shapes=[pltpu.VMEM((dk, dv), F32)],
        compiler_params=_compiler_params(("parallel", "parallel", "arbitrary"), blocks,
                                         _nbytes((dk, dv), F32) * 4 + _nbytes((c, dv), F32) * 8),
        name="retention",
    )(qk, qk, vg, vg, dmask, qdec, kdec, cdec)


def _diff_attention_body(q_ref, k_ref, v_ref, lam_ref, g_ref, o_ref, max_ref, sum_ref, acc_ref, s_ref, p_ref,
                         *, tile, strip, lambda_init):
    qi = pl.program_id(2)
    d = DIFF_HEAD_DIM
    lanes = V7X_LANES
    contract_last = (((1,), (1,)), ((), ()))

    half = tile // 2

    def row_groups(diagonal):
        if diagonal:
            return [(slice(0, half), half), (slice(half, tile), tile)]
        return [(slice(0, tile), tile)]

    def scores(kt, diagonal):
        base = pl.multiple_of(kt * tile, tile)
        for c in range(2):
            for rows, width in row_groups(diagonal):
                s_ref[c, rows, :width] = lax.dot_general(
                    q_ref[rows, c * d:(c + 1) * d], k_ref[pl.ds(base, width), c * d:(c + 1) * d],
                    contract_last, preferred_element_type=F32)
        return base

    def block(c, r, cb, diagonal):
        if diagonal and cb * lanes >= (r + 1) * strip:
            return None
        s = s_ref[c, r * strip:(r + 1) * strip, cb * lanes:(cb + 1) * lanes]
        if diagonal and (cb + 1) * lanes - 1 > r * strip:
            row_id = r * strip + lax.broadcasted_iota(jnp.int32, (strip, lanes), 0)
            col_id = cb * lanes + lax.broadcasted_iota(jnp.int32, (strip, lanes), 1)
            s = jnp.where(row_id >= col_id, s, -jnp.inf)
        return s

    def max_sweep(kt, diagonal):
        scores(kt, diagonal)
        for c in range(2):
            for r in range(tile // strip):
                rows = slice(r * strip, (r + 1) * strip)
                blocks = [block(c, r, cb, diagonal) for cb in range(tile // lanes)]
                max_ref[c, rows] = functools.reduce(jnp.maximum,
                                                    [max_ref[c, rows]] + [b for b in blocks if b is not None])

    def sum_sweep(kt, diagonal):
        base = scores(kt, diagonal)
        for c in range(2):
            for r in range(tile // strip):
                rows = slice(r * strip, (r + 1) * strip)
                row_max = max_ref[c, rows]
                total = sum_ref[c, rows]
                read_cols = half if diagonal and (r + 1) * strip <= half else tile
                for cb in range(read_cols // lanes):
                    s = block(c, r, cb, diagonal)
                    if s is None:
                        p_ref[c, rows, cb * lanes:(cb + 1) * lanes] = jnp.zeros((strip, lanes), BF16)
                        continue
                    p = jnp.exp2(s - row_max)
                    total = total + p
                    p_ref[c, rows, cb * lanes:(cb + 1) * lanes] = p.astype(BF16)
                sum_ref[c, rows] = total
        for c in range(2):
            for rows, width in row_groups(diagonal):
                acc_ref[c, rows] += jnp.dot(p_ref[c, rows, :width], v_ref[pl.ds(base, width), :],
                                            preferred_element_type=F32)

    def loop(sweep):
        def body(kt, carry):
            sweep(kt, False)
            return carry
        lax.fori_loop(0, qi, body, 0)
        sweep(qi, True)

    max_ref[...] = jnp.full_like(max_ref, -jnp.inf)
    sum_ref[...] = jnp.zeros_like(sum_ref)
    acc_ref[...] = jnp.zeros_like(acc_ref)
    loop(max_sweep)
    for c in range(2):
        max_ref[c] = jnp.broadcast_to(jnp.max(max_ref[c], axis=1, keepdims=True), (tile, lanes))
    loop(sum_sweep)

    lam = lam_ref[...]
    lam_full = (jnp.exp(jnp.sum(lam[0:1] * lam[1:2], axis=1, keepdims=True))
                - jnp.exp(jnp.sum(lam[2:3] * lam[3:4], axis=1, keepdims=True)) + lambda_init)
    inv_l = [1.0 / jnp.sum(sum_ref[c], axis=1, keepdims=True) for c in range(2)]
    out = acc_ref[0] * inv_l[0] - lam_full * (acc_ref[1] * inv_l[1])
    y = out * lax.rsqrt(jnp.mean(out * out, axis=-1, keepdims=True) + LN_EPS)
    o_ref[...] = ((y * g_ref[...]) * (1.0 - lambda_init)).astype(o_ref.dtype)


def diff_attention(q, k, v, lam, subln_g, *, layer, batch, seq, lambda_init, tile=ATT_TILE, strip=ATT_STRIP):
    h, dv = DIFF_HEADS, DIFF_V_DIM
    tile = min(tile, seq)
    assert seq % tile == 0 and (tile // 2) % strip == 0 and (tile // 2) % V7X_LANES == 0
    nq = seq // tile
    blocks = (2 * _nbytes((tile, dv), BF16) + 2 * _nbytes((seq, dv), BF16) + _nbytes((8, V7X_LANES), F32)
              + _nbytes((8, dv), F32))
    scratch = [pltpu.VMEM((2, tile, V7X_LANES), F32), pltpu.VMEM((2, tile, V7X_LANES), F32),
               pltpu.VMEM((2, tile, dv), F32), pltpu.VMEM((2, tile, tile), F32), pltpu.VMEM((2, tile, tile), BF16)]
    scratch_bytes = (2 * 2 * _nbytes((tile, V7X_LANES), F32) + 2 * _nbytes((tile, dv), F32)
                     + 2 * _nbytes((tile, tile), F32) + 2 * _nbytes((tile, tile), BF16))
    return pl.pallas_call(
        functools.partial(_diff_attention_body, tile=tile, strip=strip, lambda_init=lambda_init),
        grid=(batch, h, nq),
        in_specs=[pl.BlockSpec((tile, dv), lambda b, hh, qi: (b * nq + qi, hh)),
                  pl.BlockSpec((seq, dv), lambda b, hh, qi: (b, hh)),
                  pl.BlockSpec((seq, dv), lambda b, hh, qi: (b, hh)),
                  pl.BlockSpec((None, 4, DIFF_HEAD_DIM), lambda b, hh, qi: (layer, 0, 0)),
                  pl.BlockSpec((None, 1, dv), lambda b, hh, qi: (layer, 0, 0))],
        out_specs=pl.BlockSpec((tile, dv), lambda b, hh, qi: (b * nq + qi, hh)),
        out_shape=jax.ShapeDtypeStruct((batch * seq, h * dv), BF16),
        scratch_shapes=scratch,
        compiler_params=_compiler_params(("parallel", "parallel", "arbitrary"), blocks,
                                         scratch_bytes + 2 * _nbytes((tile, tile), F32)),
        name="diff_attention",
    )(q, k, v, lam, subln_g)


def _router_body(a_ref, w_ref, o_ref):
    logits = jnp.dot(a_ref[...], w_ref[...], preferred_element_type=F32)
    lane = lax.broadcasted_iota(jnp.int32, logits.shape, 1).astype(F32)
    logits = jnp.where(lane < N_EXPERTS, logits, -jnp.inf)
    top1 = jnp.max(logits, axis=1, keepdims=True)
    idx1 = jnp.min(jnp.where(logits == top1, lane, float(V7X_LANES)), axis=1, keepdims=True)
    rest = jnp.where(lane == idx1, -jnp.inf, logits)
    top2 = jnp.max(rest, axis=1, keepdims=True)
    idx2 = jnp.min(jnp.where(rest == top2, lane, float(V7X_LANES)), axis=1, keepdims=True)
    e = jnp.exp(top2 - top1)
    w1 = 1.0 / (1.0 + e)
    w2 = e / (1.0 + e)
    o_ref[...] = jnp.where(lane == idx1, w1, 0.0) + jnp.where(lane == idx2, w2, 0.0)


def router_gates(a, w_router_padded, *, layer, bm=MM_BM):
    m, k = a.shape
    bm = min(bm, m)
    assert m % bm == 0
    blocks = _nbytes((bm, k), a.dtype) + _nbytes((k, V7X_LANES), BF16) + _nbytes((bm, V7X_LANES), F32)
    return pl.pallas_call(
        _router_body,
        grid=(m // bm,),
        in_specs=[pl.BlockSpec((bm, k), lambda i: (i, 0)),
                  pl.BlockSpec((None, k, V7X_LANES), lambda i: (layer, 0, 0))],
        out_specs=pl.BlockSpec((bm, V7X_LANES), lambda i: (i, 0)),
        out_shape=jax.ShapeDtypeStruct((m, V7X_LANES), F32),
        compiler_params=_compiler_params(("parallel",), blocks, 8 * _nbytes((bm, V7X_LANES), F32)),
        name="router_gates",
    )(a, w_router_padded)


def _rope_tables(seq, dim):
    inv_freq = 1.0 / (ROPE_THETA ** (jnp.arange(0, dim, 2, dtype=F32) / dim))
    ang = jnp.arange(seq, dtype=F32)[:, None] * inv_freq[None, :]
    cos, sin = jnp.cos(ang), jnp.sin(ang)
    return jnp.concatenate([cos, cos], axis=-1), jnp.concatenate([-sin, sin], axis=-1)


def kernel(x, p, ret_w_in, ret_w_out, kv_w, diff_w_q, diff_w_out, diff_lambda, diff_subln_g, ffn_w13, ffn_w2,
           moe_router, moe_w13, moe_w2, ple_w_gate, ple_w_proj, ln_g, ln_b):
    batch, seq, d = x.shape
    assert d == D_MODEL
    tokens = batch * seq
    cos_r, sin_r = _rope_tables(seq, RET_QK_DIM)
    cos_d, sin_d = _rope_tables(seq, DIFF_HEAD_DIM)
    ret_tabs = retention_tables()

    w_in0 = ret_w_in[0].astype(BF16)
    ple_w_proj = ple_w_proj.astype(BF16)
    router_w = jnp.pad(moe_router.astype(BF16), ((0, 0), (0, 0), (0, V7X_LANES - N_EXPERTS)))
    flat = lambda w: w.reshape(-1, w.shape[-1])
    n_ret, n_dense, n_moe = ret_w_out.shape[0], ffn_w13.shape[0], moe_w13.shape[0]
    ln_g = ln_g.reshape(DEPTH * 3, 1, D_MODEL)
    ln_b = ln_b.reshape(DEPTH * 3, 1, D_MODEL)
    subln_g = diff_subln_g.reshape(-1, 1, DIFF_V_DIM)
    p = p.reshape(DEPTH, tokens, PLE_DIM)

    x = x.reshape(tokens, D_MODEL)
    xb = x.astype(BF16)
    qk_w = RET_HEADS * RET_QK_DIM
    v_w = RET_HEADS * RET_V_DIM
    kd = 2 * DIFF_HEADS * DIFF_HEAD_DIM
    vd = DIFF_HEADS * DIFF_V_DIM
    q_scale = DIFF_HEAD_DIM ** -0.5 * LOG2_E

    def retention_layer(i, w_in, w_out, x, xb, vg_cast, out_cast):
        qk = matmul_rope(xb, w_in, cos_r, sin_r, col_off=0, n_out=2 * qk_w, head_dim=RET_QK_DIM,
                         split_col=qk_w, scale_lo=1.0, scale_hi=RET_QK_DIM ** -0.5)
        vg, cast_a = matmul(xb, w_in, col_off=2 * qk_w, n_out=2 * v_w, out_dtype=BF16, cast=vg_cast)
        w_out = w_out if w_out is not None else cast_a.reshape(ret_w_out.shape)
        gated = retention(qk, vg, ret_tabs, batch=batch, seq=seq)
        y, cast_b = matmul_residual(gated, w_out, x, layer=i, cast=out_cast)
        return y, w_out, cast_a, cast_b

    def attention_layer(j, w_q, w_out, k_sh, v_sh, x, xb, out_cast):
        lambda_init = 0.8 - 0.6 * math.exp(-0.3 * (j + N_A_LAYERS))
        q = matmul_rope(xb, w_q, cos_d, sin_d, layer=j, n_out=kd, head_dim=DIFF_HEAD_DIM,
                        scale_lo=q_scale, scale_hi=q_scale)
        att = diff_attention(q, k_sh, v_sh, diff_lambda, subln_g, layer=j, batch=batch, seq=seq,
                             lambda_init=lambda_init)
        return matmul_residual(att, w_out, x, layer=j, cast=out_cast)

    y, w_ret_out, _, w13_d0 = retention_layer(0, w_in0, None, x, xb, (flat(ret_w_out), 0, 1),
                                               (flat(ffn_w13), 0, n_dense))
    x, xb = layer_norm(y, ln_g, ln_b, row=0)
    hidden, w2_d = matmul_swiglu(xb, w13_d0, cast=(flat(ffn_w2), 0, 1))
    w2_d = w2_d.reshape(ffn_w2.shape)
    y, w_ple = matmul_residual(hidden, w2_d, x, layer=0, cast=(flat(ple_w_gate), 0, 1))
    w_ple = w_ple.reshape(ple_w_gate.shape)
    x, xb = layer_norm(y, ln_g, ln_b, row=1)
    y, w_in1 = matmul_ple_residual(xb, w_ple, p, ple_w_proj, x, layer=0, cast=(flat(ret_w_in), 1, n_ret))
    x, xb = layer_norm(y, ln_g, ln_b, row=2)

    y, _, w13_m0, w2_m = retention_layer(1, w_in1, w_ret_out, x, xb, (flat(moe_w13), 0, n_moe),
                                         (flat(moe_w2), 0, 1))
    w13_m0 = w13_m0.reshape(moe_w13.shape[1:])
    w2_m = w2_m.reshape(n_moe, N_EXPERTS * D_EXPERT, D_MODEL)
    x, xb = layer_norm(y, ln_g, ln_b, row=3)
    gates = router_gates(xb, router_w, layer=0)
    hidden, w_kv = matmul_moe_swiglu(xb, w13_m0, gates, cast=(kv_w, 0, 1))
    y, w_q = matmul_residual(hidden, w2_m, x, layer=0, cast=(flat(diff_w_q), 0, 1))
    w_q = w_q.reshape(diff_w_q.shape)
    x, xb = layer_norm(y, ln_g, ln_b, row=4)
    y, w_att_out = matmul_ple_residual(xb, w_ple, p, ple_w_proj, x, layer=1, cast=(flat(diff_w_out), 0, 1))
    w_att_out = w_att_out.reshape(diff_w_out.shape)
    x, xb = layer_norm(y, ln_g, ln_b, row=5)
    k_sh = matmul_rope(xb, w_kv, cos_d, sin_d, col_off=0, n_out=kd, head_dim=DIFF_HEAD_DIM)
    v_sh = matmul(xb, w_kv, col_off=kd, n_out=vd, out_dtype=BF16)

    y, w13_d1 = attention_layer(0, w_q, w_att_out, k_sh, v_sh, x, xb, (flat(ffn_w13), 1, n_dense))
    x, xb = layer_norm(y, ln_g, ln_b, row=6)
    hidden, w13_m1 = matmul_swiglu(xb, w13_d1, cast=(flat(moe_w13), 1, n_moe))
    w13_m1 = w13_m1.reshape(moe_w13.shape[1:])
    y = matmul_residual(hidden, w2_d, x, layer=1)
    x, xb = layer_norm(y, ln_g, ln_b, row=7)
    y = matmul_ple_residual(xb, w_ple, p, ple_w_proj, x, layer=2)
    x, xb = layer_norm(y, ln_g, ln_b, row=8)

    y = attention_layer(1, w_q, w_att_out, k_sh, v_sh, x, xb, None)
    x, xb = layer_norm(y, ln_g, ln_b, row=9)
    gates = router_gates(xb, router_w, layer=1)
    hidden = matmul_moe_swiglu(xb, w13_m1, gates)
    y = matmul_residual(hidden, w2_m, x, layer=1)
    x, xb = layer_norm(y, ln_g, ln_b, row=10)
    y = matmul_ple_residual(xb, w_ple, p, ple_w_proj, x, layer=3)
    out = layer_norm(y, ln_g, ln_b, row=11, final=True)
    return out.reshape(batch, seq, D_MODEL)
```

```python
import functools
import math
from typing import NamedTuple

import jax
import jax.numpy as jnp
from jax import lax
from jax.experimental import pallas as pl
from jax.experimental.pallas import tpu as pltpu

F32 = jnp.float32
BF16 = jnp.bfloat16

D_MODEL = 4096
DEPTH = 4
N_A_LAYERS = DEPTH // 2
RET_HEADS = 16
RET_QK_DIM = D_MODEL // RET_HEADS
RET_V_DIM = 2 * D_MODEL // RET_HEADS
RET_CHUNK = 128
DIFF_HEADS = 16
DIFF_HEAD_DIM = D_MODEL // (2 * DIFF_HEADS)
DIFF_V_DIM = 2 * DIFF_HEAD_DIM
ROPE_THETA = 10000.0
D_FF = 2 * D_MODEL
N_EXPERTS = 8
TOP_K = 2
D_EXPERT = D_MODEL // 4
PLE_DIM = 256
LN_EPS = 1e-5
DEEPNORM_ALPHA = (2.0 * DEPTH) ** 0.25

V7X_LANES = 128
V7X_VMEM_BYTES = 64 * 1024 * 1024
V7X_VMEM_CAP = V7X_VMEM_BYTES - 6 * 1024 * 1024

MM_BM = 1024
MM_BN_WIDE_K = 256
MM_BN = 512
MM_BN_WIDE = 1024
LN_ROWS = 512
RET_ROWS = 2048
ATT_TILE = 1024
ATT_STRIP = 64
LOG2_E = math.log2(math.e)


def _nbytes(shape, dtype):
    return math.prod(shape) * jnp.dtype(dtype).itemsize


def _compiler_params(semantics, block_bytes, temp_bytes):
    need = 2 * block_bytes + temp_bytes + (2 << 20)
    return pltpu.CompilerParams(
        dimension_semantics=semantics,
        vmem_limit_bytes=int(min(max(need, 16 << 20), V7X_VMEM_CAP)),
    )


def _w_spec(w, layer, k, bn, col_block_off=0):
    if w.ndim == 3:
        return pl.BlockSpec((None, k, bn), lambda i, j: (layer, 0, j + col_block_off))
    return pl.BlockSpec((k, bn), lambda i, j: (0, j + col_block_off))


def _hosting_cast(body, n_in):
    def hosted(*refs):
        body(*refs[:n_in], refs[n_in + 1])
        refs[n_in + 2][...] = refs[n_in][...].astype(BF16)
    return hosted


def _launch_matmul(body, name, grid, in_specs, args, out_block, out_shape, block_bytes, temp_bytes, cast=None):
    out_spec = pl.BlockSpec(out_block, lambda i, j: (i, j))
    semantics = ("parallel", "arbitrary")
    if cast is None:
        return pl.pallas_call(
            body, grid=grid, in_specs=in_specs, out_specs=out_spec, out_shape=out_shape,
            compiler_params=_compiler_params(semantics, block_bytes, temp_bytes), name=name,
        )(*args)
    src, part, n_parts = cast
    rows, cols = src.shape[0] // n_parts, src.shape[1]
    steps = grid[0] * grid[1]
    slab = rows // steps
    assert src.shape[0] % n_parts == 0 and rows % steps == 0 and slab % 16 == 0
    block_bytes += _nbytes((slab, cols), F32) + _nbytes((slab, cols), BF16)
    return pl.pallas_call(
        _hosting_cast(body, len(in_specs)),
        grid=grid,
        in_specs=list(in_specs) + [pl.BlockSpec((slab, cols), lambda i, j: (part * steps + i * grid[1] + j, 0))],
        out_specs=[out_spec, pl.BlockSpec((slab, cols), lambda i, j: (i * grid[1] + j, 0))],
        out_shape=[out_shape, jax.ShapeDtypeStruct((rows, cols), BF16)],
        compiler_params=_compiler_params(semantics, block_bytes, temp_bytes), name=name + "_cast",
    )(*args, src)


def _mm_plain_body(a_ref, w_ref, o_ref):
    o_ref[...] = jnp.dot(a_ref[...], w_ref[...], preferred_element_type=F32).astype(o_ref.dtype)


def matmul(a, w, *, layer=None, col_off=0, n_out, out_dtype, cast=None):
    m, k = a.shape
    bm, bn = MM_BM, (MM_BN_WIDE if k <= D_MODEL else MM_BN_WIDE_K)
    bm = min(bm, m)
    assert m % bm == 0 and n_out % bn == 0 and col_off % bn == 0
    blocks = _nbytes((bm, k), a.dtype) + _nbytes((k, bn), w.dtype) + _nbytes((bm, bn), out_dtype)
    return _launch_matmul(
        _mm_plain_body, "matmul", (m // bm, n_out // bn),
        [pl.BlockSpec((bm, k), lambda i, j: (i, 0)), _w_spec(w, layer, k, bn, col_off // bn)], (a, w),
        (bm, bn), jax.ShapeDtypeStruct((m, n_out), out_dtype), blocks, 2 * _nbytes((bm, bn), F32), cast)


class Normed(NamedTuple):
    y: jax.Array
    mean: jax.Array
    rstd: jax.Array
    gain: jax.Array
    bias: jax.Array
    row: int


def _residual_operands(res, bm, bn):
    tile = pl.BlockSpec((bm, bn), lambda i, j: (i, j))
    if not isinstance(res, Normed):
        return [tile], (res,), _nbytes((bm, bn), F32)
    stat = pl.BlockSpec((bm, V7X_LANES), lambda i, j: (i, 0))
    vec = pl.BlockSpec((None, 1, bn), lambda i, j: (res.row, 0, j))
    nbytes = _nbytes((bm, bn), F32) + 2 * _nbytes((bm, V7X_LANES), F32) + 2 * _nbytes((8, bn), F32)
    return [tile, stat, stat, vec, vec], (res.y, res.mean, res.rstd, res.gain, res.bias), nbytes


def _residual_tile(refs):
    if len(refs) == 1:
        return refs[0][...]
    y_ref, mean_ref, rstd_ref, gain_ref, bias_ref = refs
    reps = y_ref.shape[1] // V7X_LANES
    mean = jnp.concatenate([mean_ref[...]] * reps, axis=1)
    rstd = jnp.concatenate([rstd_ref[...]] * reps, axis=1)
    return (y_ref[...] - mean) * rstd * gain_ref[...] + bias_ref[...]


def _mm_residual_body(a_ref, w_ref, *refs):
    *res_refs, o_ref = refs
    o_ref[...] = (DEEPNORM_ALPHA * _residual_tile(res_refs)
                  + jnp.dot(a_ref[...], w_ref[...], preferred_element_type=F32))


def matmul_residual(a, w, res, *, layer, cast=None):
    m, k = a.shape
    n = w.shape[-1]
    bm, bn = MM_BM, (MM_BN if k <= D_MODEL else MM_BN_WIDE_K)
    bm = min(bm, m)
    assert m % bm == 0 and n % bn == 0
    res_specs, res_args, res_bytes = _residual_operands(res, bm, bn)
    blocks = _nbytes((bm, k), a.dtype) + _nbytes((k, bn), w.dtype) + _nbytes((bm, bn), F32) + res_bytes
    return _launch_matmul(
        _mm_residual_body, "matmul_residual", (m // bm, n // bn),
        [pl.BlockSpec((bm, k), lambda i, j: (i, 0)), _w_spec(w, layer, k, bn)] + res_specs, (a, w) + res_args,
        (bm, bn), jax.ShapeDtypeStruct((m, n), F32), blocks, 3 * _nbytes((bm, bn), F32), cast)


def _mm_rope_body(a_ref, w_ref, cos_ref, sin_ref, o_ref, *, head_dim, split_block, scale_lo, scale_hi):
    acc = jnp.dot(a_ref[...], w_ref[...], preferred_element_type=F32)
    cos = cos_ref[...]
    sin = sin_ref[...]
    if scale_lo == scale_hi:
        scale = scale_lo
    else:
        scale = jnp.where(pl.program_id(1) < split_block, scale_lo, scale_hi)
    for h in range(acc.shape[1] // head_dim):
        cols = slice(h * head_dim, (h + 1) * head_dim)
        xb = acc[:, cols]
        rot = pltpu.roll(xb, head_dim // 2, axis=1)
        o_ref[:, cols] = ((xb * cos + rot * sin) * scale).astype(o_ref.dtype)


def matmul_rope(a, w, cos_full, sin_signed, *, layer=None, col_off=0, n_out, head_dim,
                split_col=0, scale_lo=1.0, scale_hi=1.0, bm=MM_BM, bn=MM_BN_WIDE):
    m, k = a.shape
    seq = cos_full.shape[0]
    bm = min(bm, seq)
    assert m % bm == 0 and seq % bm == 0 and n_out % bn == 0 and bn % head_dim == 0
    assert col_off % bn == 0 and split_col % bn == 0
    pos_tiles = seq // bm
    blocks = (_nbytes((bm, k), a.dtype) + _nbytes((k, bn), w.dtype) + _nbytes((bm, bn), BF16)
              + 2 * _nbytes((bm, head_dim), F32))
    body = functools.partial(_mm_rope_body, head_dim=head_dim, split_block=split_col // bn,
                             scale_lo=scale_lo, scale_hi=scale_hi)
    return _launch_matmul(
        body, "matmul_rope", (m // bm, n_out // bn),
        [pl.BlockSpec((bm, k), lambda i, j: (i, 0)),
         _w_spec(w, layer, k, bn, col_off // bn),
         pl.BlockSpec((bm, head_dim), lambda i, j: (i % pos_tiles, 0)),
         pl.BlockSpec((bm, head_dim), lambda i, j: (i % pos_tiles, 0))], (a, w, cos_full, sin_signed),
        (bm, bn), jax.ShapeDtypeStruct((m, n_out), BF16), blocks, 3 * _nbytes((bm, bn), F32))


def _mm_swiglu_body(a_ref, wa_ref, wb_ref, o_ref):
    a = a_ref[...]
    ga = jnp.dot(a, wa_ref[...], preferred_element_type=F32)
    gb = jnp.dot(a, wb_ref[...], preferred_element_type=F32)
    o_ref[...] = (jax.nn.silu(ga) * gb).astype(o_ref.dtype)


def matmul_swiglu(a, w13, *, layer=None, bm=MM_BM, bn=MM_BN, cast=None):
    m, k = a.shape
    f = w13.shape[-1] // 2
    bm = min(bm, m)
    assert m % bm == 0 and f % bn == 0
    blocks = _nbytes((bm, k), a.dtype) + 2 * _nbytes((k, bn), w13.dtype) + _nbytes((bm, bn), BF16)
    return _launch_matmul(
        _mm_swiglu_body, "matmul_swiglu", (m // bm, f // bn),
        [pl.BlockSpec((bm, k), lambda i, j: (i, 0)), _w_spec(w13, layer, k, bn, 0),
         _w_spec(w13, layer, k, bn, f // bn)], (a, w13, w13),
        (bm, bn), jax.ShapeDtypeStruct((m, f), BF16), blocks, 4 * _nbytes((bm, bn), F32), cast)


def _mm_moe_swiglu_body(a_ref, wa_ref, wb_ref, gates_ref, o_ref, *, blocks_per_expert):
    a = a_ref[...]
    ga = jnp.dot(a, wa_ref[...], preferred_element_type=F32)
    gb = jnp.dot(a, wb_ref[...], preferred_element_type=F32)
    expert = pl.program_id(1) // blocks_per_expert
    gates = gates_ref[...]
    lane = lax.broadcasted_iota(jnp.int32, gates.shape, 1)
    gate = jnp.sum(jnp.where(lane == expert, gates, 0.0), axis=1, keepdims=True)
    o_ref[...] = (gate * (jax.nn.silu(ga) * gb)).astype(o_ref.dtype)


def matmul_moe_swiglu(a, w13, gates, *, bm=MM_BM, bn=MM_BN, cast=None):
    m, k = a.shape
    n_exp, f = w13.shape[0], w13.shape[-1] // 2
    bm = min(bm, m)
    assert m % bm == 0 and f % bn == 0
    bpe = f // bn
    blocks = (_nbytes((bm, k), a.dtype) + 2 * _nbytes((k, bn), w13.dtype) + _nbytes((bm, bn), BF16)
              + _nbytes((bm, V7X_LANES), F32))
    return _launch_matmul(
        functools.partial(_mm_moe_swiglu_body, blocks_per_expert=bpe), "matmul_moe_swiglu",
        (m // bm, n_exp * bpe),
        [pl.BlockSpec((bm, k), lambda i, j: (i, 0)),
         pl.BlockSpec((None, k, bn), lambda i, j: (j // bpe, 0, j % bpe)),
         pl.BlockSpec((None, k, bn), lambda i, j: (j // bpe, 0, j % bpe + bpe)),
         pl.BlockSpec((bm, V7X_LANES), lambda i, j: (i, 0))], (a, w13, w13, gates),
        (bm, bn), jax.ShapeDtypeStruct((m, n_exp * f), BF16), blocks, 4 * _nbytes((bm, bn), F32), cast)


def _mm_ple_body(a_ref, wg_ref, p_ref, wp_ref, *refs):
    *res_refs, o_ref = refs
    gate = jnp.dot(a_ref[...], wg_ref[...], preferred_element_type=F32)
    proj = jnp.dot(p_ref[...].astype(BF16), wp_ref[...], preferred_element_type=F32)
    o_ref[...] = DEEPNORM_ALPHA * _residual_tile(res_refs) + jax.nn.sigmoid(gate) * proj


def matmul_ple_residual(a, w_gate, p, w_proj, res, *, layer, bm=MM_BM, bn=MM_BN, cast=None):
    m, k = a.shape
    n = w_gate.shape[-1]
    kp = p.shape[-1]
    bm = min(bm, m)
    assert m % bm == 0 and n % bn == 0
    res_specs, res_args, res_bytes = _residual_operands(res, bm, bn)
    blocks = (_nbytes((bm, k), a.dtype) + _nbytes((k, bn), w_gate.dtype) + _nbytes((bm, kp), p.dtype)
              + _nbytes((kp, bn), w_proj.dtype) + _nbytes((bm, bn), F32) + res_bytes)
    return _launch_matmul(
        _mm_ple_body, "matmul_ple_residual", (m // bm, n // bn),
        [pl.BlockSpec((bm, k), lambda i, j: (i, 0)),
         _w_spec(w_gate, layer, k, bn),
         pl.BlockSpec((None, bm, kp), lambda i, j: (layer, i, 0)),
         _w_spec(w_proj, layer, kp, bn)] + res_specs, (a, w_gate, p, w_proj) + res_args,
        (bm, bn), jax.ShapeDtypeStruct((m, n), F32), blocks, 4 * _nbytes((bm, bn), F32), cast)


def _ln_normalise(y_ref, g_ref, b_ref):
    y = y_ref[...]
    mean = jnp.mean(y, axis=-1, keepdims=True)
    d = y - mean
    rstd = lax.rsqrt(jnp.mean(d * d, axis=-1, keepdims=True) + LN_EPS)
    return d * rstd * g_ref[...] + b_ref[...], mean, rstd


def _ln_stats_body(y_ref, g_ref, b_ref, ob_ref, mean_ref, rstd_ref):
    out, mean, rstd = _ln_normalise(y_ref, g_ref, b_ref)
    ob_ref[...] = out.astype(BF16)
    mean_ref[...] = jnp.broadcast_to(mean, mean_ref.shape)
    rstd_ref[...] = jnp.broadcast_to(rstd, rstd_ref.shape)


def _ln_f32_body(y_ref, g_ref, b_ref, o_ref):
    o_ref[...] = _ln_normalise(y_ref, g_ref, b_ref)[0]


def layer_norm(y, ln_g, ln_b, *, row, final=False, rows=LN_ROWS):
    m, d = y.shape
    rows = min(rows, m)
    assert m % rows == 0
    tile = pl.BlockSpec((rows, d), lambda i: (i, 0))
    stat = pl.BlockSpec((rows, V7X_LANES), lambda i: (i, 0))
    vec = pl.BlockSpec((None, 1, d), lambda i: (row, 0, 0))
    blocks = _nbytes((rows, d), F32) + 2 * _nbytes((8, d), F32)
    if final:
        return pl.pallas_call(
            _ln_f32_body, grid=(m // rows,), in_specs=[tile, vec, vec], out_specs=tile,
            out_shape=jax.ShapeDtypeStruct((m, d), F32),
            compiler_params=_compiler_params(("parallel",), blocks + _nbytes((rows, d), F32),
                                             3 * _nbytes((rows, d), F32)),
            name="layer_norm_f32",
        )(y, ln_g, ln_b)
    xb, mean, rstd = pl.pallas_call(
        _ln_stats_body, grid=(m // rows,), in_specs=[tile, vec, vec], out_specs=[tile, stat, stat],
        out_shape=[jax.ShapeDtypeStruct((m, d), BF16), jax.ShapeDtypeStruct((m, V7X_LANES), F32),
                   jax.ShapeDtypeStruct((m, V7X_LANES), F32)],
        compiler_params=_compiler_params(("parallel",), blocks + _nbytes((rows, d), BF16)
                                         + 2 * _nbytes((rows, V7X_LANES), F32), 3 * _nbytes((rows, d), F32)),
        name="layer_norm",
    )(y, ln_g, ln_b)
    return Normed(y, mean, rstd, ln_g, ln_b, row), xb


def _retention_body(q_ref, k_ref, v_ref, g_ref, dmask_ref, qdec_ref, kdec_ref, cdec_ref, o_ref, state_ref,
                    *, n_chunks, chunk):
    @pl.when(pl.program_id(2) == 0)
    def _():
        state_ref[...] = jnp.zeros_like(state_ref)

    dmask = dmask_ref[...]
    qdec = qdec_ref[...]
    kdec = kdec_ref[...]
    cdec = cdec_ref[...]
    for c in range(n_chunks):
        rows = pl.ds(c * chunk, chunk)
        q = q_ref[rows, :]
        k = k_ref[rows, :]
        v = v_ref[rows, :]
        state = state_ref[...]
        scores = lax.dot_general(q, k, (((1,), (1,)), ((), ())), preferred_element_type=F32) * dmask
        inner = jnp.dot(scores.astype(BF16), v, preferred_element_type=F32)
        cross = jnp.dot((q.astype(F32) * qdec).astype(BF16), state.astype(BF16), preferred_element_type=F32)
        k_scaled = (k.astype(F32) * kdec).astype(BF16)
        update = lax.dot_general(k_scaled, v, (((0,), (0,)), ((), ())), preferred_element_type=F32)
        state_ref[...] = state * cdec + update
        out = inner + cross
        mu = jnp.mean(out, axis=-1, keepdims=True)
        d = out - mu
        var = jnp.mean(d * d, axis=-1, keepdims=True)
        normed = d * lax.rsqrt(var + LN_EPS)
        gate = g_ref[rows, :].astype(F32)
        o_ref[rows, :] = (jax.nn.silu(gate) * normed).astype(o_ref.dtype)


def retention_tables():
    h, c = RET_HEADS, RET_CHUNK
    log_gamma = jnp.log1p(-jnp.exp2(-5.0 - jnp.arange(h, dtype=F32)))
    pos = jnp.arange(c, dtype=F32)
    rel = pos[:, None] - pos[None, :]
    dmask = jnp.where(rel >= 0, jnp.exp(log_gamma[:, None, None] * jnp.maximum(rel, 0.0)), 0.0)
    qdec = jnp.exp(log_gamma[:, None] * (pos + 1.0))[:, :, None]
    kdec = jnp.exp(log_gamma[:, None] * (c - 1.0 - pos))[:, :, None]
    cdec = jnp.exp(log_gamma * c)[:, None, None]
    return (dmask,
            jnp.broadcast_to(qdec, (h, c, RET_QK_DIM)),
            jnp.broadcast_to(kdec, (h, c, RET_QK_DIM)),
            jnp.broadcast_to(cdec, (h, 1, RET_V_DIM)))


def retention(qk, vg, tables, *, batch, seq, rows=RET_ROWS):
    h, dk, dv, c = RET_HEADS, RET_QK_DIM, RET_V_DIM, RET_CHUNK
    rows = min(rows, seq)
    assert seq % rows == 0 and rows % c == 0
    steps = seq // rows
    dmask, qdec, kdec, cdec = tables
    row_tile = lambda b, hh, l: b * steps + l
    blocks = (2 * _nbytes((rows, dk), BF16) + 3 * _nbytes((rows, dv), BF16) + _nbytes((c, c), F32)
              + 2 * _nbytes((c, dk), F32) + _nbytes((8, dv), F32))
    return pl.pallas_call(
        functools.partial(_retention_body, n_chunks=rows // c, chunk=c),
        grid=(batch, h, steps),
        in_specs=[pl.BlockSpec((rows, dk), lambda b, hh, l: (row_tile(b, hh, l), hh)),
                  pl.BlockSpec((rows, dk), lambda b, hh, l: (row_tile(b, hh, l), h + hh)),
                  pl.BlockSpec((rows, dv), lambda b, hh, l: (row_tile(b, hh, l), hh)),
                  pl.BlockSpec((rows, dv), lambda b, hh, l: (row_tile(b, hh, l), h + hh)),
                  pl.BlockSpec((None, c, c), lambda b, hh, l: (hh, 0, 0)),
                  pl.BlockSpec((None, c, dk), lambda b, hh, l: (hh, 0, 0)),
                  pl.BlockSpec((None, c, dk), lambda b, hh, l: (hh, 0, 0)),
                  pl.BlockSpec((None, 1, dv), lambda b, hh, l: (hh, 0, 0))],
        out_specs=pl.BlockSpec((rows, dv), lambda b, hh, l: (row_tile(b, hh, l), hh)),
        out_shape=jax.ShapeDtypeStruct((batch * seq, h * dv), BF16),
        scratch_shapes=[pltpu.VMEM((dk, dv), F32)],
        compiler_params=_compiler_params(("parallel", "parallel", "arbitrary"), blocks,
                                         _nbytes((dk, dv), F32) * 4 + _nbytes((rows, dv), F32) * 4),
        name="retention",
    )(qk, qk, vg, vg, dmask, qdec, kdec, cdec)


def _diff_attention_body(q_ref, k_ref, v_ref, lam_ref, g_ref, o_ref, max_ref, sum_ref, acc_ref, s_ref, p_ref,
                         *, tile, strip, lambda_init):
    qi = pl.program_id(2)
    d = DIFF_HEAD_DIM
    lanes = V7X_LANES
    contract_last = (((1,), (1,)), ((), ()))

    half = tile // 2

    def row_groups(diagonal):
        if diagonal:
            return [(slice(0, half), half), (slice(half, tile), tile)]
        return [(slice(0, tile), tile)]

    def scores(kt, diagonal):
        base = pl.multiple_of(kt * tile, tile)
        for c in range(2):
            for rows, width in row_groups(diagonal):
                s_ref[c, rows, :width] = lax.dot_general(
                    q_ref[rows, c * d:(c + 1) * d], k_ref[pl.ds(base, width), c * d:(c + 1) * d],
                    contract_last, preferred_element_type=F32)
        return base

    def block(c, r, cb, diagonal):
        if diagonal and cb * lanes >= (r + 1) * strip:
            return None
        s = s_ref[c, r * strip:(r + 1) * strip, cb * lanes:(cb + 1) * lanes]
        if diagonal and (cb + 1) * lanes - 1 > r * strip:
            row_id = r * strip + lax.broadcasted_iota(jnp.int32, (strip, lanes), 0)
            col_id = cb * lanes + lax.broadcasted_iota(jnp.int32, (strip, lanes), 1)
            s = jnp.where(row_id >= col_id, s, -jnp.inf)
        return s

    def max_sweep(kt, diagonal):
        scores(kt, diagonal)
        for c in range(2):
            for r in range(tile // strip):
                rows = slice(r * strip, (r + 1) * strip)
                blocks = [block(c, r, cb, diagonal) for cb in range(tile // lanes)]
                max_ref[c, rows] = functools.reduce(jnp.maximum,
                                                    [max_ref[c, rows]] + [b for b in blocks if b is not None])

    def sum_sweep(kt, diagonal):
        base = scores(kt, diagonal)
        for c in range(2):
            for r in range(tile // strip):
                rows = slice(r * strip, (r + 1) * strip)
                row_max = max_ref[c, rows]
                total = sum_ref[c, rows]
                read_cols = half if diagonal and (r + 1) * strip <= half else tile
                for cb in range(read_cols // lanes):
                    s = block(c, r, cb, diagonal)
                    if s is None:
                        p_ref[c, rows, cb * lanes:(cb + 1) * lanes] = jnp.zeros((strip, lanes), BF16)
                        continue
                    p = jnp.exp2(s - row_max)
                    total = total + p
                    p_ref[c, rows, cb * lanes:(cb + 1) * lanes] = p.astype(BF16)
                sum_ref[c, rows] = total
        for c in range(2):
            for rows, width in row_groups(diagonal):
                acc_ref[c, rows] += jnp.dot(p_ref[c, rows, :width], v_ref[pl.ds(base, width), :],
                                            preferred_element_type=F32)

    def loop(sweep):
        def body(kt, carry):
            sweep(kt, False)
            return carry
        lax.fori_loop(0, qi, body, 0)
        sweep(qi, True)

    max_ref[...] = jnp.full_like(max_ref, -jnp.inf)
    sum_ref[...] = jnp.zeros_like(sum_ref)
    acc_ref[...] = jnp.zeros_like(acc_ref)
    loop(max_sweep)
    for c in range(2):
        max_ref[c] = jnp.broadcast_to(jnp.max(max_ref[c], axis=1, keepdims=True), (tile, lanes))
    loop(sum_sweep)

    lam = lam_ref[...]
    lam_full = (jnp.exp(jnp.sum(lam[0:1] * lam[1:2], axis=1, keepdims=True))
                - jnp.exp(jnp.sum(lam[2:3] * lam[3:4], axis=1, keepdims=True)) + lambda_init)
    inv_l = [1.0 / jnp.sum(sum_ref[c], axis=1, keepdims=True) for c in range(2)]
    out = acc_ref[0] * inv_l[0] - lam_full * (acc_ref[1] * inv_l[1])
    y = out * lax.rsqrt(jnp.mean(out * out, axis=-1, keepdims=True) + LN_EPS)
    o_ref[...] = ((y * g_ref[...]) * (1.0 - lambda_init)).astype(o_ref.dtype)


def diff_attention(q, k, v, lam, subln_g, *, layer, batch, seq, lambda_init, tile=ATT_TILE, strip=ATT_STRIP):
    h, dv = DIFF_HEADS, DIFF_V_DIM
    tile = min(tile, seq)
    assert seq % tile == 0 and (tile // 2) % strip == 0 and (tile // 2) % V7X_LANES == 0
    nq = seq // tile
    blocks = (2 * _nbytes((tile, dv), BF16) + 2 * _nbytes((seq, dv), BF16) + _nbytes((8, V7X_LANES), F32)
              + _nbytes((8, dv), F32))
    scratch = [pltpu.VMEM((2, tile, V7X_LANES), F32), pltpu.VMEM((2, tile, V7X_LANES), F32),
               pltpu.VMEM((2, tile, dv), F32), pltpu.VMEM((2, tile, tile), F32), pltpu.VMEM((2, tile, tile), BF16)]
    scratch_bytes = (2 * 2 * _nbytes((tile, V7X_LANES), F32) + 2 * _nbytes((tile, dv), F32)
                     + 2 * _nbytes((tile, tile), F32) + 2 * _nbytes((tile, tile), BF16))
    return pl.pallas_call(
        functools.partial(_diff_attention_body, tile=tile, strip=strip, lambda_init=lambda_init),
        grid=(batch, h, nq),
        in_specs=[pl.BlockSpec((tile, dv), lambda b, hh, qi: (b * nq + qi, hh)),
                  pl.BlockSpec((seq, dv), lambda b, hh, qi: (b, hh)),
                  pl.BlockSpec((seq, dv), lambda b, hh, qi: (b, hh)),
                  pl.BlockSpec((None, 4, DIFF_HEAD_DIM), lambda b, hh, qi: (layer, 0, 0)),
                  pl.BlockSpec((None, 1, dv), lambda b, hh, qi: (layer, 0, 0))],
        out_specs=pl.BlockSpec((tile, dv), lambda b, hh, qi: (b * nq + qi, hh)),
        out_shape=jax.ShapeDtypeStruct((batch * seq, h * dv), BF16),
        scratch_shapes=scratch,
        compiler_params=_compiler_params(("parallel", "parallel", "arbitrary"), blocks,
                                         scratch_bytes + 2 * _nbytes((tile, tile), F32)),
        name="diff_attention",
    )(q, k, v, lam, subln_g)


def _router_body(a_ref, w_ref, o_ref):
    logits = jnp.dot(a_ref[...], w_ref[...], preferred_element_type=F32)
    lane = lax.broadcasted_iota(jnp.int32, logits.shape, 1).astype(F32)
    logits = jnp.where(lane < N_EXPERTS, logits, -jnp.inf)
    top1 = jnp.max(logits, axis=1, keepdims=True)
    idx1 = jnp.min(jnp.where(logits == top1, lane, float(V7X_LANES)), axis=1, keepdims=True)
    rest = jnp.where(lane == idx1, -jnp.inf, logits)
    top2 = jnp.max(rest, axis=1, keepdims=True)
    idx2 = jnp.min(jnp.where(rest == top2, lane, float(V7X_LANES)), axis=1, keepdims=True)
    e = jnp.exp(top2 - top1)
    w1 = 1.0 / (1.0 + e)
    w2 = e / (1.0 + e)
    o_ref[...] = jnp.where(lane == idx1, w1, 0.0) + jnp.where(lane == idx2, w2, 0.0)


def router_gates(a, w_router_padded, *, layer, bm=MM_BM):
    m, k = a.shape
    bm = min(bm, m)
    assert m % bm == 0
    blocks = _nbytes((bm, k), a.dtype) + _nbytes((k, V7X_LANES), BF16) + _nbytes((bm, V7X_LANES), F32)
    return pl.pallas_call(
        _router_body,
        grid=(m // bm,),
        in_specs=[pl.BlockSpec((bm, k), lambda i: (i, 0)),
                  pl.BlockSpec((None, k, V7X_LANES), lambda i: (layer, 0, 0))],
        out_specs=pl.BlockSpec((bm, V7X_LANES), lambda i: (i, 0)),
        out_shape=jax.ShapeDtypeStruct((m, V7X_LANES), F32),
        compiler_params=_compiler_params(("parallel",), blocks, 8 * _nbytes((bm, V7X_LANES), F32)),
        name="router_gates",
    )(a, w_router_padded)


def _rope_tables(seq, dim):
    inv_freq = 1.0 / (ROPE_THETA ** (jnp.arange(0, dim, 2, dtype=F32) / dim))
    ang = jnp.arange(seq, dtype=F32)[:, None] * inv_freq[None, :]
    cos, sin = jnp.cos(ang), jnp.sin(ang)
    return jnp.concatenate([cos, cos], axis=-1), jnp.concatenate([-sin, sin], axis=-1)


def kernel(x, p, ret_w_in, ret_w_out, kv_w, diff_w_q, diff_w_out, diff_lambda, diff_subln_g, ffn_w13, ffn_w2,
           moe_router, moe_w13, moe_w2, ple_w_gate, ple_w_proj, ln_g, ln_b):
    batch, seq, d = x.shape
    assert d == D_MODEL
    tokens = batch * seq
    cos_r, sin_r = _rope_tables(seq, RET_QK_DIM)
    cos_d, sin_d = _rope_tables(seq, DIFF_HEAD_DIM)
    ret_tabs = retention_tables()

    w_in0 = ret_w_in[0].astype(BF16)
    ple_w_proj = ple_w_proj.astype(BF16)
    router_w = jnp.pad(moe_router.astype(BF16), ((0, 0), (0, 0), (0, V7X_LANES - N_EXPERTS)))
    flat = lambda w: w.reshape(-1, w.shape[-1])
    n_ret, n_dense, n_moe = ret_w_out.shape[0], ffn_w13.shape[0], moe_w13.shape[0]
    ln_g = ln_g.reshape(DEPTH * 3, 1, D_MODEL)
    ln_b = ln_b.reshape(DEPTH * 3, 1, D_MODEL)
    subln_g = diff_subln_g.reshape(-1, 1, DIFF_V_DIM)
    p = p.reshape(DEPTH, tokens, PLE_DIM)

    x = x.reshape(tokens, D_MODEL)
    xb = x.astype(BF16)
    qk_w = RET_HEADS * RET_QK_DIM
    v_w = RET_HEADS * RET_V_DIM
    kd = 2 * DIFF_HEADS * DIFF_HEAD_DIM
    vd = DIFF_HEADS * DIFF_V_DIM
    q_scale = DIFF_HEAD_DIM ** -0.5 * LOG2_E

    def retention_layer(i, w_in, w_out, x, xb, vg_cast, out_cast):
        qk = matmul_rope(xb, w_in, cos_r, sin_r, col_off=0, n_out=2 * qk_w, head_dim=RET_QK_DIM,
                         split_col=qk_w, scale_lo=1.0, scale_hi=RET_QK_DIM ** -0.5)
        vg, cast_a = matmul(xb, w_in, col_off=2 * qk_w, n_out=2 * v_w, out_dtype=BF16, cast=vg_cast)
        w_out = w_out if w_out is not None else cast_a.reshape(ret_w_out.shape)
        gated = retention(qk, vg, ret_tabs, batch=batch, seq=seq)
        y, cast_b = matmul_residual(gated, w_out, x, layer=i, cast=out_cast)
        return y, w_out, cast_a, cast_b

    def attention_layer(j, w_q, w_out, k_sh, v_sh, x, xb, out_cast):
        lambda_init = 0.8 - 0.6 * math.exp(-0.3 * (j + N_A_LAYERS))
        q = matmul_rope(xb, w_q, cos_d, sin_d, layer=j, n_out=kd, head_dim=DIFF_HEAD_DIM,
                        scale_lo=q_scale, scale_hi=q_scale)
        att = diff_attention(q, k_sh, v_sh, diff_lambda, subln_g, layer=j, batch=batch, seq=seq,
                             lambda_init=lambda_init)
        return matmul_residual(att, w_out, x, layer=j, cast=out_cast)

    y, w_ret_out, _, w13_d0 = retention_layer(0, w_in0, None, x, xb, (flat(ret_w_out), 0, 1),
                                               (flat(ffn_w13), 0, n_dense))
    x, xb = layer_norm(y, ln_g, ln_b, row=0)
    hidden, w2_d = matmul_swiglu(xb, w13_d0, cast=(flat(ffn_w2), 0, 1))
    w2_d = w2_d.reshape(ffn_w2.shape)
    y, w_ple = matmul_residual(hidden, w2_d, x, layer=0, cast=(flat(ple_w_gate), 0, 1))
    w_ple = w_ple.reshape(ple_w_gate.shape)
    x, xb = layer_norm(y, ln_g, ln_b, row=1)
    y, w_in1 = matmul_ple_residual(xb, w_ple, p, ple_w_proj, x, layer=0, cast=(flat(ret_w_in), 1, n_ret))
    x, xb = layer_norm(y, ln_g, ln_b, row=2)

    y, _, w13_m0, w2_m = retention_layer(1, w_in1, w_ret_out, x, xb, (flat(moe_w13), 0, n_moe),
                                         (flat(moe_w2), 0, 1))
    w13_m0 = w13_m0.reshape(moe_w13.shape[1:])
    w2_m = w2_m.reshape(n_moe, N_EXPERTS * D_EXPERT, D_MODEL)
    x, xb = layer_norm(y, ln_g, ln_b, row=3)
    gates = router_gates(xb, router_w, layer=0)
    hidden, w_kv = matmul_moe_swiglu(xb, w13_m0, gates, cast=(kv_w, 0, 1))
    y, w_q = matmul_residual(hidden, w2_m, x, layer=0, cast=(flat(diff_w_q), 0, 1))
    w_q = w_q.reshape(diff_w_q.shape)
    x, xb = layer_norm(y, ln_g, ln_b, row=4)
    y, w_att_out = matmul_ple_residual(xb, w_ple, p, ple_w_proj, x, layer=1, cast=(flat(diff_w_out), 0, 1))
    w_att_out = w_att_out.reshape(diff_w_out.shape)
    x, xb = layer_norm(y, ln_g, ln_b, row=5)
    k_sh = matmul_rope(xb, w_kv, cos_d, sin_d, col_off=0, n_out=kd, head_dim=DIFF_HEAD_DIM)
    v_sh = matmul(xb, w_kv, col_off=kd, n_out=vd, out_dtype=BF16)

    y, w13_d1 = attention_layer(0, w_q, w_att_out, k_sh, v_sh, x, xb, (flat(ffn_w13), 1, n_dense))
    x, xb = layer_norm(y, ln_g, ln_b, row=6)
    hidden, w13_m1 = matmul_swiglu(xb, w13_d1, cast=(flat(moe_w13), 1, n_moe))
    w13_m1 = w13_m1.reshape(moe_w13.shape[1:])
    y = matmul_residual(hidden, w2_d, x, layer=1)
    x, xb = layer_norm(y, ln_g, ln_b, row=7)
    y = matmul_ple_residual(xb, w_ple, p, ple_w_proj, x, layer=2)
    x, xb = layer_norm(y, ln_g, ln_b, row=8)

    y = attention_layer(1, w_q, w_att_out, k_sh, v_sh, x, xb, None)
    x, xb = layer_norm(y, ln_g, ln_b, row=9)
    gates = router_gates(xb, router_w, layer=1)
    hidden = matmul_moe_swiglu(xb, w13_m1, gates)
    y = matmul_residual(hidden, w2_m, x, layer=1)
    x, xb = layer_norm(y, ln_g, ln_b, row=10)
    y = matmul_ple_residual(xb, w_ple, p, ple_w_proj, x, layer=3)
    out = layer_norm(y, ln_g, ln_b, row=11, final=True)
    return out.reshape(batch, seq, D_MODEL)
```

```python
import functools
import math
from typing import NamedTuple

import jax
import jax.numpy as jnp
from jax import lax
from jax.experimental import pallas as pl
from jax.experimental.pallas import tpu as pltpu

F32 = jnp.float32
BF16 = jnp.bfloat16

D_MODEL = 4096
DEPTH = 4
N_A_LAYERS = DEPTH // 2
RET_HEADS = 16
RET_QK_DIM = D_MODEL // RET_HEADS
RET_V_DIM = 2 * D_MODEL // RET_HEADS
RET_CHUNK = 128
DIFF_HEADS = 16
DIFF_HEAD_DIM = D_MODEL // (2 * DIFF_HEADS)
DIFF_V_DIM = 2 * DIFF_HEAD_DIM
ROPE_THETA = 10000.0
N_EXPERTS = 8
D_EXPERT = D_MODEL // 4
PLE_DIM = 256
LN_EPS = 1e-5
DEEPNORM_ALPHA = (2.0 * DEPTH) ** 0.25

V7X_LANES = 128
V7X_VMEM_BYTES = 64 * 1024 * 1024
V7X_VMEM_CAP = V7X_VMEM_BYTES - 6 * 1024 * 1024
V7X_VMEM_FLOOR = 16 * 1024 * 1024
V7X_BF16_SUBLANES = 16
VMEM_SLACK = 2 * 1024 * 1024

MM_BM = 1024
MM_BN_WIDE_K = 256
MM_BN = 512
MM_BN_WIDE = 1024
LN_ROWS = 512
RET_ROWS = 2048
ATT_TILE = 1024
ATT_STRIP = 64
ATT_SCORE_SLOTS = 3
LOG2_E = math.log2(math.e)


def _nbytes(shape, dtype):
    return math.prod(shape) * jnp.dtype(dtype).itemsize


def _compiler_params(semantics, block_bytes, temp_bytes):
    need = 2 * block_bytes + temp_bytes + VMEM_SLACK
    return pltpu.CompilerParams(
        dimension_semantics=semantics,
        vmem_limit_bytes=int(min(max(need, V7X_VMEM_FLOOR), V7X_VMEM_CAP)),
    )


def _w_spec(w, layer, k, bn, col_block_off=0):
    if w.ndim == 3:
        return pl.BlockSpec((None, k, bn), lambda i, j: (layer, 0, j + col_block_off))
    return pl.BlockSpec((k, bn), lambda i, j: (0, j + col_block_off))


def _hosting_cast(body, n_in):
    def hosted(*refs):
        body(*refs[:n_in], refs[n_in + 1])
        refs[n_in + 2][...] = refs[n_in][...].astype(BF16)
    return hosted


def _launch_matmul(body, name, grid, in_specs, args, out_block, out_shape, block_bytes, temp_bytes, cast=None):
    out_spec = pl.BlockSpec(out_block, lambda i, j: (i, j))
    semantics = ("parallel", "arbitrary")
    if cast is None:
        return pl.pallas_call(
            body, grid=grid, in_specs=in_specs, out_specs=out_spec, out_shape=out_shape,
            compiler_params=_compiler_params(semantics, block_bytes, temp_bytes), name=name,
        )(*args)
    src, part, n_parts = cast
    rows, cols = src.shape[0] // n_parts, src.shape[1]
    steps = grid[0] * grid[1]
    slab = rows // steps
    assert src.shape[0] % n_parts == 0 and rows % steps == 0 and slab % V7X_BF16_SUBLANES == 0
    block_bytes += _nbytes((slab, cols), F32) + _nbytes((slab, cols), BF16)
    return pl.pallas_call(
        _hosting_cast(body, len(in_specs)),
        grid=grid,
        in_specs=list(in_specs) + [pl.BlockSpec((slab, cols), lambda i, j: (part * steps + i * grid[1] + j, 0))],
        out_specs=[out_spec, pl.BlockSpec((slab, cols), lambda i, j: (i * grid[1] + j, 0))],
        out_shape=[out_shape, jax.ShapeDtypeStruct((rows, cols), BF16)],
        compiler_params=_compiler_params(semantics, block_bytes, temp_bytes), name=name + "_cast",
    )(*args, src)


def _mm_plain_body(a_ref, w_ref, o_ref):
    o_ref[...] = jnp.dot(a_ref[...], w_ref[...], preferred_element_type=F32).astype(o_ref.dtype)


def matmul(a, w, *, layer=None, col_off=0, n_out, out_dtype, cast=None):
    m, k = a.shape
    bm, bn = MM_BM, (MM_BN_WIDE if k <= D_MODEL else MM_BN_WIDE_K)
    bm = min(bm, m)
    assert m % bm == 0 and n_out % bn == 0 and col_off % bn == 0
    blocks = _nbytes((bm, k), a.dtype) + _nbytes((k, bn), w.dtype) + _nbytes((bm, bn), out_dtype)
    return _launch_matmul(
        _mm_plain_body, "matmul", (m // bm, n_out // bn),
        [pl.BlockSpec((bm, k), lambda i, j: (i, 0)), _w_spec(w, layer, k, bn, col_off // bn)], (a, w),
        (bm, bn), jax.ShapeDtypeStruct((m, n_out), out_dtype), blocks, 2 * _nbytes((bm, bn), F32), cast)


class Normed(NamedTuple):
    y: jax.Array
    mean: jax.Array
    rstd: jax.Array
    gain: jax.Array
    bias: jax.Array
    row: int


def _residual_operands(res, bm, bn):
    tile = pl.BlockSpec((bm, bn), lambda i, j: (i, j))
    if not isinstance(res, Normed):
        return [tile], (res,), _nbytes((bm, bn), F32)
    stat = pl.BlockSpec((bm, V7X_LANES), lambda i, j: (i, 0))
    vec = pl.BlockSpec((None, 1, bn), lambda i, j: (res.row, 0, j))
    nbytes = _nbytes((bm, bn), F32) + 2 * _nbytes((bm, V7X_LANES), F32) + 2 * _nbytes((8, bn), F32)
    return [tile, stat, stat, vec, vec], (res.y, res.mean, res.rstd, res.gain, res.bias), nbytes


def _residual_tile(refs):
    if len(refs) == 1:
        return refs[0][...]
    y_ref, mean_ref, rstd_ref, gain_ref, bias_ref = refs
    reps = y_ref.shape[1] // V7X_LANES
    mean = jnp.concatenate([mean_ref[...]] * reps, axis=1)
    rstd = jnp.concatenate([rstd_ref[...]] * reps, axis=1)
    return (y_ref[...] - mean) * rstd * gain_ref[...] + bias_ref[...]


def _mm_residual_body(a_ref, w_ref, *refs):
    *res_refs, o_ref = refs
    o_ref[...] = (DEEPNORM_ALPHA * _residual_tile(res_refs)
                  + jnp.dot(a_ref[...], w_ref[...], preferred_element_type=F32))


def matmul_residual(a, w, res, *, layer, cast=None):
    m, k = a.shape
    n = w.shape[-1]
    bm, bn = MM_BM, (MM_BN if k <= D_MODEL else MM_BN_WIDE_K)
    bm = min(bm, m)
    assert m % bm == 0 and n % bn == 0
    res_specs, res_args, res_bytes = _residual_operands(res, bm, bn)
    blocks = _nbytes((bm, k), a.dtype) + _nbytes((k, bn), w.dtype) + _nbytes((bm, bn), F32) + res_bytes
    return _launch_matmul(
        _mm_residual_body, "matmul_residual", (m // bm, n // bn),
        [pl.BlockSpec((bm, k), lambda i, j: (i, 0)), _w_spec(w, layer, k, bn)] + res_specs, (a, w) + res_args,
        (bm, bn), jax.ShapeDtypeStruct((m, n), F32), blocks, 3 * _nbytes((bm, bn), F32), cast)


def _mm_rope_body(a_ref, w_ref, cos_ref, sin_ref, o_ref, *, head_dim, split_block, scale_lo, scale_hi):
    acc = jnp.dot(a_ref[...], w_ref[...], preferred_element_type=F32)
    cos = cos_ref[...]
    sin = sin_ref[...]
    if scale_lo == scale_hi:
        scale = scale_lo
    else:
        scale = jnp.where(pl.program_id(1) < split_block, scale_lo, scale_hi)
    for h in range(acc.shape[1] // head_dim):
        cols = slice(h * head_dim, (h + 1) * head_dim)
        xb = acc[:, cols]
        rot = pltpu.roll(xb, head_dim // 2, axis=1)
        o_ref[:, cols] = ((xb * cos + rot * sin) * scale).astype(o_ref.dtype)


def matmul_rope(a, w, cos_full, sin_signed, *, layer=None, col_off=0, n_out, head_dim,
                split_col=0, scale_lo=1.0, scale_hi=1.0, bm=MM_BM, bn=MM_BN_WIDE):
    m, k = a.shape
    seq = cos_full.shape[0]
    bm = min(bm, seq)
    assert m % bm == 0 and seq % bm == 0 and n_out % bn == 0 and bn % head_dim == 0
    assert col_off % bn == 0 and split_col % bn == 0
    pos_tiles = seq // bm
    blocks = (_nbytes((bm, k), a.dtype) + _nbytes((k, bn), w.dtype) + _nbytes((bm, bn), BF16)
              + 2 * _nbytes((bm, head_dim), F32))
    body = functools.partial(_mm_rope_body, head_dim=head_dim, split_block=split_col // bn,
                             scale_lo=scale_lo, scale_hi=scale_hi)
    return _launch_matmul(
        body, "matmul_rope", (m // bm, n_out // bn),
        [pl.BlockSpec((bm, k), lambda i, j: (i, 0)),
         _w_spec(w, layer, k, bn, col_off // bn),
         pl.BlockSpec((bm, head_dim), lambda i, j: (i % pos_tiles, 0)),
         pl.BlockSpec((bm, head_dim), lambda i, j: (i % pos_tiles, 0))], (a, w, cos_full, sin_signed),
        (bm, bn), jax.ShapeDtypeStruct((m, n_out), BF16), blocks, 3 * _nbytes((bm, bn), F32))


def _mm_swiglu_body(a_ref, wa_ref, wb_ref, o_ref):
    a = a_ref[...]
    ga = jnp.dot(a, wa_ref[...], preferred_element_type=F32)
    gb = jnp.dot(a, wb_ref[...], preferred_element_type=F32)
    o_ref[...] = (jax.nn.silu(ga) * gb).astype(o_ref.dtype)


def matmul_swiglu(a, w13, *, layer=None, bm=MM_BM, bn=MM_BN, cast=None):
    m, k = a.shape
    f = w13.shape[-1] // 2
    bm = min(bm, m)
    assert m % bm == 0 and f % bn == 0
    blocks = _nbytes((bm, k), a.dtype) + 2 * _nbytes((k, bn), w13.dtype) + _nbytes((bm, bn), BF16)
    return _launch_matmul(
        _mm_swiglu_body, "matmul_swiglu", (m // bm, f // bn),
        [pl.BlockSpec((bm, k), lambda i, j: (i, 0)), _w_spec(w13, layer, k, bn, 0),
         _w_spec(w13, layer, k, bn, f // bn)], (a, w13, w13),
        (bm, bn), jax.ShapeDtypeStruct((m, f), BF16), blocks, 4 * _nbytes((bm, bn), F32), cast)


def _mm_moe_swiglu_body(a_ref, wa_ref, wb_ref, gates_ref, o_ref, *, blocks_per_expert):
    a = a_ref[...]
    ga = jnp.dot(a, wa_ref[...], preferred_element_type=F32)
    gb = jnp.dot(a, wb_ref[...], preferred_element_type=F32)
    expert = pl.program_id(1) // blocks_per_expert
    gates = gates_ref[...]
    lane = lax.broadcasted_iota(jnp.int32, gates.shape, 1)
    gate = jnp.sum(jnp.where(lane == expert, gates, 0.0), axis=1, keepdims=True)
    o_ref[...] = (gate * (jax.nn.silu(ga) * gb)).astype(o_ref.dtype)


def matmul_moe_swiglu(a, w13, gates, *, bm=MM_BM, bn=MM_BN, cast=None):
    m, k = a.shape
    n_exp, f = w13.shape[0], w13.shape[-1] // 2
    bm = min(bm, m)
    assert m % bm == 0 and f % bn == 0
    bpe = f // bn
    blocks = (_nbytes((bm, k), a.dtype) + 2 * _nbytes((k, bn), w13.dtype) + _nbytes((bm, bn), BF16)
              + _nbytes((bm, V7X_LANES), F32))
    return _launch_matmul(
        functools.partial(_mm_moe_swiglu_body, blocks_per_expert=bpe), "matmul_moe_swiglu",
        (m // bm, n_exp * bpe),
        [pl.BlockSpec((bm, k), lambda i, j: (i, 0)),
         pl.BlockSpec((None, k, bn), lambda i, j: (j // bpe, 0, j % bpe)),
         pl.BlockSpec((None, k, bn), lambda i, j: (j // bpe, 0, j % bpe + bpe)),
         pl.BlockSpec((bm, V7X_LANES), lambda i, j: (i, 0))], (a, w13, w13, gates),
        (bm, bn), jax.ShapeDtypeStruct((m, n_exp * f), BF16), blocks, 4 * _nbytes((bm, bn), F32), cast)


def _mm_ple_body(a_ref, wg_ref, p_ref, wp_ref, *refs):
    *res_refs, o_ref = refs
    gate = jnp.dot(a_ref[...], wg_ref[...], preferred_element_type=F32)
    proj = jnp.dot(p_ref[...].astype(BF16), wp_ref[...], preferred_element_type=F32)
    o_ref[...] = DEEPNORM_ALPHA * _residual_tile(res_refs) + jax.nn.sigmoid(gate) * proj


def matmul_ple_residual(a, w_gate, p, w_proj, res, *, layer, bm=MM_BM, bn=MM_BN, cast=None):
    m, k = a.shape
    n = w_gate.shape[-1]
    kp = p.shape[-1]
    bm = min(bm, m)
    assert m % bm == 0 and n % bn == 0
    res_specs, res_args, res_bytes = _residual_operands(res, bm, bn)
    blocks = (_nbytes((bm, k), a.dtype) + _nbytes((k, bn), w_gate.dtype) + _nbytes((bm, kp), p.dtype)
              + _nbytes((kp, bn), w_proj.dtype) + _nbytes((bm, bn), F32) + res_bytes)
    return _launch_matmul(
        _mm_ple_body, "matmul_ple_residual", (m // bm, n // bn),
        [pl.BlockSpec((bm, k), lambda i, j: (i, 0)),
         _w_spec(w_gate, layer, k, bn),
         pl.BlockSpec((None, bm, kp), lambda i, j: (layer, i, 0)),
         _w_spec(w_proj, layer, kp, bn)] + res_specs, (a, w_gate, p, w_proj) + res_args,
        (bm, bn), jax.ShapeDtypeStruct((m, n), F32), blocks, 4 * _nbytes((bm, bn), F32), cast)


def _ln_normalise(y_ref, g_ref, b_ref):
    y = y_ref[...]
    mean = jnp.mean(y, axis=-1, keepdims=True)
    d = y - mean
    rstd = lax.rsqrt(jnp.mean(d * d, axis=-1, keepdims=True) + LN_EPS)
    return d * rstd * g_ref[...] + b_ref[...], mean, rstd


def _ln_stats_body(y_ref, g_ref, b_ref, ob_ref, mean_ref, rstd_ref):
    out, mean, rstd = _ln_normalise(y_ref, g_ref, b_ref)
    ob_ref[...] = out.astype(BF16)
    mean_ref[...] = jnp.broadcast_to(mean, mean_ref.shape)
    rstd_ref[...] = jnp.broadcast_to(rstd, rstd_ref.shape)


def _ln_f32_body(y_ref, g_ref, b_ref, o_ref):
    o_ref[...] = _ln_normalise(y_ref, g_ref, b_ref)[0]


def layer_norm(y, ln_g, ln_b, *, row, final=False, rows=LN_ROWS):
    m, d = y.shape
    rows = min(rows, m)
    assert m % rows == 0
    tile = pl.BlockSpec((rows, d), lambda i: (i, 0))
    stat = pl.BlockSpec((rows, V7X_LANES), lambda i: (i, 0))
    vec = pl.BlockSpec((None, 1, d), lambda i: (row, 0, 0))
    blocks = _nbytes((rows, d), F32) + 2 * _nbytes((8, d), F32)
    if final:
        return pl.pallas_call(
            _ln_f32_body, grid=(m // rows,), in_specs=[tile, vec, vec], out_specs=tile,
            out_shape=jax.ShapeDtypeStruct((m, d), F32),
            compiler_params=_compiler_params(("parallel",), blocks + _nbytes((rows, d), F32),
                                             3 * _nbytes((rows, d), F32)),
            name="layer_norm_f32",
        )(y, ln_g, ln_b)
    xb, mean, rstd = pl.pallas_call(
        _ln_stats_body, grid=(m // rows,), in_specs=[tile, vec, vec], out_specs=[tile, stat, stat],
        out_shape=[jax.ShapeDtypeStruct((m, d), BF16), jax.ShapeDtypeStruct((m, V7X_LANES), F32),
                   jax.ShapeDtypeStruct((m, V7X_LANES), F32)],
        compiler_params=_compiler_params(("parallel",), blocks + _nbytes((rows, d), BF16)
                                         + 2 * _nbytes((rows, V7X_LANES), F32), 3 * _nbytes((rows, d), F32)),
        name="layer_norm",
    )(y, ln_g, ln_b)
    return Normed(y, mean, rstd, ln_g, ln_b, row), xb


def _retention_body(q_ref, k_ref, v_ref, g_ref, dmask_ref, qdec_ref, kdec_ref, cdec_ref, o_ref, state_ref,
                    *, n_chunks, chunk):
    @pl.when(pl.program_id(2) == 0)
    def _():
        state_ref[...] = jnp.zeros_like(state_ref)

    dmask = dmask_ref[...]
    qdec = qdec_ref[...]
    kdec = kdec_ref[...]
    cdec = cdec_ref[...]
    for c in range(n_chunks):
        rows = pl.ds(c * chunk, chunk)
        q = q_ref[rows, :]
        k = k_ref[rows, :]
        v = v_ref[rows, :]
        state = state_ref[...]
        scores = lax.dot_general(q, k, (((1,), (1,)), ((), ())), preferred_element_type=F32) * dmask
        inner = jnp.dot(scores.astype(BF16), v, preferred_element_type=F32)
        cross = jnp.dot((q.astype(F32) * qdec).astype(BF16), state.astype(BF16), preferred_element_type=F32)
        k_scaled = (k.astype(F32) * kdec).astype(BF16)
        update = lax.dot_general(k_scaled, v, (((0,), (0,)), ((), ())), preferred_element_type=F32)
        state_ref[...] = state * cdec + update
        out = inner + cross
        mu = jnp.mean(out, axis=-1, keepdims=True)
        d = out - mu
        var = jnp.mean(d * d, axis=-1, keepdims=True)
        normed = d * lax.rsqrt(var + LN_EPS)
        gate = g_ref[rows, :].astype(F32)
        o_ref[rows, :] = (jax.nn.silu(gate) * normed).astype(o_ref.dtype)


def retention_tables():
    h, c = RET_HEADS, RET_CHUNK
    log_gamma = jnp.log1p(-jnp.exp2(-5.0 - jnp.arange(h, dtype=F32)))
    pos = jnp.arange(c, dtype=F32)
    rel = pos[:, None] - pos[None, :]
    dmask = jnp.where(rel >= 0, jnp.exp(log_gamma[:, None, None] * jnp.maximum(rel, 0.0)), 0.0)
    qdec = jnp.exp(log_gamma[:, None] * (pos + 1.0))[:, :, None]
    kdec = jnp.exp(log_gamma[:, None] * (c - 1.0 - pos))[:, :, None]
    cdec = jnp.exp(log_gamma * c)[:, None, None]
    return (dmask,
            jnp.broadcast_to(qdec, (h, c, RET_QK_DIM)),
            jnp.broadcast_to(kdec, (h, c, RET_QK_DIM)),
            jnp.broadcast_to(cdec, (h, 1, RET_V_DIM)))


def retention(qk, vg, tables, *, batch, seq, rows=RET_ROWS):
    h, dk, dv, c = RET_HEADS, RET_QK_DIM, RET_V_DIM, RET_CHUNK
    rows = min(rows, seq)
    assert seq % rows == 0 and rows % c == 0
    steps = seq // rows
    dmask, qdec, kdec, cdec = tables
    row_tile = lambda b, hh, l: b * steps + l
    blocks = (2 * _nbytes((rows, dk), BF16) + 3 * _nbytes((rows, dv), BF16) + _nbytes((c, c), F32)
              + 2 * _nbytes((c, dk), F32) + _nbytes((8, dv), F32))
    return pl.pallas_call(
        functools.partial(_retention_body, n_chunks=rows // c, chunk=c),
        grid=(batch, h, steps),
        in_specs=[pl.BlockSpec((rows, dk), lambda b, hh, l: (row_tile(b, hh, l), hh)),
                  pl.BlockSpec((rows, dk), lambda b, hh, l: (row_tile(b, hh, l), h + hh)),
                  pl.BlockSpec((rows, dv), lambda b, hh, l: (row_tile(b, hh, l), hh)),
                  pl.BlockSpec((rows, dv), lambda b, hh, l: (row_tile(b, hh, l), h + hh)),
                  pl.BlockSpec((None, c, c), lambda b, hh, l: (hh, 0, 0)),
                  pl.BlockSpec((None, c, dk), lambda b, hh, l: (hh, 0, 0)),
                  pl.BlockSpec((None, c, dk), lambda b, hh, l: (hh, 0, 0)),
                  pl.BlockSpec((None, 1, dv), lambda b, hh, l: (hh, 0, 0))],
        out_specs=pl.BlockSpec((rows, dv), lambda b, hh, l: (row_tile(b, hh, l), hh)),
        out_shape=jax.ShapeDtypeStruct((batch * seq, h * dv), BF16),
        scratch_shapes=[pltpu.VMEM((dk, dv), F32)],
        compiler_params=_compiler_params(("parallel", "parallel", "arbitrary"), blocks,
                                         _nbytes((dk, dv), F32) * 4 + _nbytes((rows, dv), F32) * 4),
        name="retention",
    )(qk, qk, vg, vg, dmask, qdec, kdec, cdec)


def _diff_attention_body(q_ref, k_ref, v_ref, lam_ref, g_ref, o_ref, max_ref, sum_ref, acc_ref, s_ref, p_ref,
                         *, tile, strip, lambda_init):
    qi = pl.program_id(2)
    d = DIFF_HEAD_DIM
    lanes = V7X_LANES
    contract_last = (((1,), (1,)), ((), ()))

    half = tile // 2

    def row_groups(diagonal):
        if diagonal:
            return [(slice(0, half), half), (slice(half, tile), tile)]
        return [(slice(0, tile), tile)]

    n_cache = s_ref.shape[0] - 1

    def scores(kt, diagonal, slot):
        base = pl.multiple_of(kt * tile, tile)
        for c in range(2):
            for rows, width in row_groups(diagonal):
                s_ref[slot, c, rows, :width] = lax.dot_general(
                    q_ref[rows, c * d:(c + 1) * d], k_ref[pl.ds(base, width), c * d:(c + 1) * d],
                    contract_last, preferred_element_type=F32)

    def block(slot, c, r, cb, diagonal):
        if diagonal and cb * lanes >= (r + 1) * strip:
            return None
        s = s_ref[slot, c, r * strip:(r + 1) * strip, cb * lanes:(cb + 1) * lanes]
        if diagonal and (cb + 1) * lanes - 1 > r * strip:
            row_id = r * strip + lax.broadcasted_iota(jnp.int32, (strip, lanes), 0)
            col_id = cb * lanes + lax.broadcasted_iota(jnp.int32, (strip, lanes), 1)
            s = jnp.where(row_id >= col_id, s, -jnp.inf)
        return s

    def max_sweep(kt, diagonal, slot):
        scores(kt, diagonal, slot)
        for c in range(2):
            for r in range(tile // strip):
                rows = slice(r * strip, (r + 1) * strip)
                blocks = [block(slot, c, r, cb, diagonal) for cb in range(tile // lanes)]
                max_ref[c, rows] = functools.reduce(jnp.maximum,
                                                    [max_ref[c, rows]] + [b for b in blocks if b is not None])

    def sum_sweep(kt, diagonal, slot, recompute):
        if recompute:
            scores(kt, diagonal, slot)
        base = pl.multiple_of(kt * tile, tile)
        for c in range(2):
            for r in range(tile // strip):
                rows = slice(r * strip, (r + 1) * strip)
                row_max = max_ref[c, rows]
                total = sum_ref[c, rows]
                read_cols = half if diagonal and (r + 1) * strip <= half else tile
                for cb in range(read_cols // lanes):
                    s = block(slot, c, r, cb, diagonal)
                    if s is None:
                        p_ref[c, rows, cb * lanes:(cb + 1) * lanes] = jnp.zeros((strip, lanes), BF16)
                        continue
                    p = jnp.exp2(s - row_max)
                    total = total + p
                    p_ref[c, rows, cb * lanes:(cb + 1) * lanes] = p.astype(BF16)
                sum_ref[c, rows] = total
        for c in range(2):
            for rows, width in row_groups(diagonal):
                acc_ref[c, rows] += jnp.dot(p_ref[c, rows, :width], v_ref[pl.ds(base, width), :],
                                            preferred_element_type=F32)

    def for_key_tiles(lo, hi, step):
        def body(kt, carry):
            step(kt)
            return carry
        lax.fori_loop(lo, hi, body, 0)

    max_ref[...] = jnp.full_like(max_ref, -jnp.inf)
    sum_ref[...] = jnp.zeros_like(sum_ref)
    acc_ref[...] = jnp.zeros_like(acc_ref)
    for_key_tiles(0, qi, lambda kt: max_sweep(kt, False, jnp.minimum(kt, n_cache)))
    max_sweep(qi, True, n_cache)
    for c in range(2):
        max_ref[c] = jnp.broadcast_to(jnp.max(max_ref[c], axis=1, keepdims=True), (tile, lanes))
    sum_sweep(qi, True, n_cache, False)
    for_key_tiles(0, jnp.minimum(qi, n_cache), lambda kt: sum_sweep(kt, False, kt, False))
    for_key_tiles(n_cache, qi, lambda kt: sum_sweep(kt, False, n_cache, True))

    lam = lam_ref[...]
    lam_full = (jnp.exp(jnp.sum(lam[0:1] * lam[1:2], axis=1, keepdims=True))
                - jnp.exp(jnp.sum(lam[2:3] * lam[3:4], axis=1, keepdims=True)) + lambda_init)
    inv_l = [1.0 / jnp.sum(sum_ref[c], axis=1, keepdims=True) for c in range(2)]
    out = acc_ref[0] * inv_l[0] - lam_full * (acc_ref[1] * inv_l[1])
    y = out * lax.rsqrt(jnp.mean(out * out, axis=-1, keepdims=True) + LN_EPS)
    o_ref[...] = ((y * g_ref[...]) * (1.0 - lambda_init)).astype(o_ref.dtype)


def diff_attention(q, k, v, lam, subln_g, *, layer, batch, seq, lambda_init, tile=ATT_TILE, strip=ATT_STRIP):
    h, dv = DIFF_HEADS, DIFF_V_DIM
    tile = min(tile, seq)
    assert seq % tile == 0 and (tile // 2) % strip == 0 and (tile // 2) % V7X_LANES == 0
    nq = seq // tile
    blocks = (2 * _nbytes((tile, dv), BF16) + 2 * _nbytes((seq, dv), BF16) + _nbytes((8, V7X_LANES), F32)
              + _nbytes((8, dv), F32))
    scratch = [pltpu.VMEM((2, tile, V7X_LANES), F32), pltpu.VMEM((2, tile, V7X_LANES), F32),
               pltpu.VMEM((2, tile, dv), F32), pltpu.VMEM((ATT_SCORE_SLOTS, 2, tile, tile), F32),
               pltpu.VMEM((2, tile, tile), BF16)]
    scratch_bytes = (2 * 2 * _nbytes((tile, V7X_LANES), F32) + 2 * _nbytes((tile, dv), F32)
                     + ATT_SCORE_SLOTS * 2 * _nbytes((tile, tile), F32) + 2 * _nbytes((tile, tile), BF16))
    return pl.pallas_call(
        functools.partial(_diff_attention_body, tile=tile, strip=strip, lambda_init=lambda_init),
        grid=(batch, h, nq),
        in_specs=[pl.BlockSpec((tile, dv), lambda b, hh, qi: (b * nq + qi, hh)),
                  pl.BlockSpec((seq, dv), lambda b, hh, qi: (b, hh)),
                  pl.BlockSpec((seq, dv), lambda b, hh, qi: (b, hh)),
                  pl.BlockSpec((None, 4, DIFF_HEAD_DIM), lambda b, hh, qi: (layer, 0, 0)),
                  pl.BlockSpec((None, 1, dv), lambda b, hh, qi: (layer, 0, 0))],
        out_specs=pl.BlockSpec((tile, dv), lambda b, hh, qi: (b * nq + qi, hh)),
        out_shape=jax.ShapeDtypeStruct((batch * seq, h * dv), BF16),
        scratch_shapes=scratch,
        compiler_params=_compiler_params(("parallel", "parallel", "arbitrary"), blocks,
                                         scratch_bytes + 3 * _nbytes((tile, tile), F32)),
        name="diff_attention",
    )(q, k, v, lam, subln_g)


def _router_body(a_ref, w_ref, o_ref):
    logits = jnp.dot(a_ref[...], w_ref[...], preferred_element_type=F32)
    lane = lax.broadcasted_iota(jnp.int32, logits.shape, 1).astype(F32)
    logits = jnp.where(lane < N_EXPERTS, logits, -jnp.inf)
    top1 = jnp.max(logits, axis=1, keepdims=True)
    idx1 = jnp.min(jnp.where(logits == top1, lane, float(V7X_LANES)), axis=1, keepdims=True)
    rest = jnp.where(lane == idx1, -jnp.inf, logits)
    top2 = jnp.max(rest, axis=1, keepdims=True)
    idx2 = jnp.min(jnp.where(rest == top2, lane, float(V7X_LANES)), axis=1, keepdims=True)
    e = jnp.exp(top2 - top1)
    w1 = 1.0 / (1.0 + e)
    w2 = e / (1.0 + e)
    o_ref[...] = jnp.where(lane == idx1, w1, 0.0) + jnp.where(lane == idx2, w2, 0.0)


def router_gates(a, w_router_padded, *, layer, bm=MM_BM):
    m, k = a.shape
    bm = min(bm, m)
    assert m % bm == 0
    blocks = _nbytes((bm, k), a.dtype) + _nbytes((k, V7X_LANES), BF16) + _nbytes((bm, V7X_LANES), F32)
    return pl.pallas_call(
        _router_body,
        grid=(m // bm,),
        in_specs=[pl.BlockSpec((bm, k), lambda i: (i, 0)),
                  pl.BlockSpec((None, k, V7X_LANES), lambda i: (layer, 0, 0))],
        out_specs=pl.BlockSpec((bm, V7X_LANES), lambda i: (i, 0)),
        out_shape=jax.ShapeDtypeStruct((m, V7X_LANES), F32),
        compiler_params=_compiler_params(("parallel",), blocks, 8 * _nbytes((bm, V7X_LANES), F32)),
        name="router_gates",
    )(a, w_router_padded)


def _rope_tables(seq, dim):
    inv_freq = 1.0 / (ROPE_THETA ** (jnp.arange(0, dim, 2, dtype=F32) / dim))
    ang = jnp.arange(seq, dtype=F32)[:, None] * inv_freq[None, :]
    cos, sin = jnp.cos(ang), jnp.sin(ang)
    return jnp.concatenate([cos, cos], axis=-1), jnp.concatenate([-sin, sin], axis=-1)


def kernel(x, p, ret_w_in, ret_w_out, kv_w, diff_w_q, diff_w_out, diff_lambda, diff_subln_g, ffn_w13, ffn_w2,
           moe_router, moe_w13, moe_w2, ple_w_gate, ple_w_proj, ln_g, ln_b):
    batch, seq, d = x.shape
    assert d == D_MODEL
    tokens = batch * seq
    cos_r, sin_r = _rope_tables(seq, RET_QK_DIM)
    cos_d, sin_d = _rope_tables(seq, DIFF_HEAD_DIM)
    ret_tabs = retention_tables()

    w_in0 = ret_w_in[0].astype(BF16)
    ple_w_proj = ple_w_proj.astype(BF16)
    router_w = jnp.pad(moe_router.astype(BF16), ((0, 0), (0, 0), (0, V7X_LANES - N_EXPERTS)))
    flat = lambda w: w.reshape(-1, w.shape[-1])
    n_ret, n_dense, n_moe = ret_w_out.shape[0], ffn_w13.shape[0], moe_w13.shape[0]
    ln_g = ln_g.reshape(DEPTH * 3, 1, D_MODEL)
    ln_b = ln_b.reshape(DEPTH * 3, 1, D_MODEL)
    subln_g = diff_subln_g.reshape(-1, 1, DIFF_V_DIM)
    p = p.reshape(DEPTH, tokens, PLE_DIM)

    x = x.reshape(tokens, D_MODEL)
    xb = x.astype(BF16)
    qk_w = RET_HEADS * RET_QK_DIM
    v_w = RET_HEADS * RET_V_DIM
    kd = 2 * DIFF_HEADS * DIFF_HEAD_DIM
    vd = DIFF_HEADS * DIFF_V_DIM
    q_scale = DIFF_HEAD_DIM ** -0.5 * LOG2_E

    def retention_layer(i, w_in, w_out, x, xb, vg_cast, out_cast):
        qk = matmul_rope(xb, w_in, cos_r, sin_r, col_off=0, n_out=2 * qk_w, head_dim=RET_QK_DIM,
                         split_col=qk_w, scale_lo=1.0, scale_hi=RET_QK_DIM ** -0.5)
        vg, cast_a = matmul(xb, w_in, col_off=2 * qk_w, n_out=2 * v_w, out_dtype=BF16, cast=vg_cast)
        w_out = w_out if w_out is not None else cast_a.reshape(ret_w_out.shape)
        gated = retention(qk, vg, ret_tabs, batch=batch, seq=seq)
        y, cast_b = matmul_residual(gated, w_out, x, layer=i, cast=out_cast)
        return y, w_out, cast_a, cast_b

    def attention_layer(j, w_q, w_out, k_sh, v_sh, x, xb, out_cast):
        lambda_init = 0.8 - 0.6 * math.exp(-0.3 * (j + N_A_LAYERS))
        q = matmul_rope(xb, w_q, cos_d, sin_d, layer=j, n_out=kd, head_dim=DIFF_HEAD_DIM,
                        scale_lo=q_scale, scale_hi=q_scale)
        att = diff_attention(q, k_sh, v_sh, diff_lambda, subln_g, layer=j, batch=batch, seq=seq,
                             lambda_init=lambda_init)
        return matmul_residual(att, w_out, x, layer=j, cast=out_cast)

    y, w_ret_out, _, w13_d0 = retention_layer(0, w_in0, None, x, xb, (flat(ret_w_out), 0, 1),
                                               (flat(ffn_w13), 0, n_dense))
    x, xb = layer_norm(y, ln_g, ln_b, row=0)
    hidden, w2_d = matmul_swiglu(xb, w13_d0, cast=(flat(ffn_w2), 0, 1))
    w2_d = w2_d.reshape(ffn_w2.shape)
    y, w_ple = matmul_residual(hidden, w2_d, x, layer=0, cast=(flat(ple_w_gate), 0, 1))
    w_ple = w_ple.reshape(ple_w_gate.shape)
    x, xb = layer_norm(y, ln_g, ln_b, row=1)
    y, w_in1 = matmul_ple_residual(xb, w_ple, p, ple_w_proj, x, layer=0, cast=(flat(ret_w_in), 1, n_ret))
    x, xb = layer_norm(y, ln_g, ln_b, row=2)

    y, _, w13_m0, w2_m = retention_layer(1, w_in1, w_ret_out, x, xb, (flat(moe_w13), 0, n_moe),
                                         (flat(moe_w2), 0, 1))
    w13_m0 = w13_m0.reshape(moe_w13.shape[1:])
    w2_m = w2_m.reshape(n_moe, N_EXPERTS * D_EXPERT, D_MODEL)
    x, xb = layer_norm(y, ln_g, ln_b, row=3)
    gates = router_gates(xb, router_w, layer=0)
    hidden, w_kv = matmul_moe_swiglu(xb, w13_m0, gates, cast=(kv_w, 0, 1))
    y, w_q = matmul_residual(hidden, w2_m, x, layer=0, cast=(flat(diff_w_q), 0, 1))
    w_q = w_q.reshape(diff_w_q.shape)
    x, xb = layer_norm(y, ln_g, ln_b, row=4)
    y, w_att_out = matmul_ple_residual(xb, w_ple, p, ple_w_proj, x, layer=1, cast=(flat(diff_w_out), 0, 1))
    w_att_out = w_att_out.reshape(diff_w_out.shape)
    x, xb = layer_norm(y, ln_g, ln_b, row=5)
    k_sh = matmul_rope(xb, w_kv, cos_d, sin_d, col_off=0, n_out=kd, head_dim=DIFF_HEAD_DIM)
    v_sh = matmul(xb, w_kv, col_off=kd, n_out=vd, out_dtype=BF16)

    y, w13_d1 = attention_layer(0, w_q, w_att_out, k_sh, v_sh, x, xb, (flat(ffn_w13), 1, n_dense))
    x, xb = layer_norm(y, ln_g, ln_b, row=6)
    hidden, w13_m1 = matmul_swiglu(xb, w13_d1, cast=(flat(moe_w13), 1, n_moe))
    w13_m1 = w13_m1.reshape(moe_w13.shape[1:])
    y = matmul_residual(hidden, w2_d, x, layer=1)
    x, xb = layer_norm(y, ln_g, ln_b, row=7)
    y = matmul_ple_residual(xb, w_ple, p, ple_w_proj, x, layer=2)
    x, xb = layer_norm(y, ln_g, ln_b, row=8)

    y = attention_layer(1, w_q, w_att_out, k_sh, v_sh, x, xb, None)
    x, xb = layer_norm(y, ln_g, ln_b, row=9)
    gates = router_gates(xb, router_w, layer=1)
    hidden = matmul_moe_swiglu(xb, w13_m1, gates)
    y = matmul_residual(hidden, w2_m, x, layer=1)
    x, xb = layer_norm(y, ln_g, ln_b, row=10)
    y = matmul_ple_residual(xb, w_ple, p, ple_w_proj, x, layer=3)
    out = layer_norm(y, ln_g, ln_b, row=11, final=True)
    return out.reshape(batch, seq, D_MODEL)
```

```python
import functools
import math
from typing import NamedTuple

import jax
import jax.numpy as jnp
from jax import lax
from jax.experimental import pallas as pl
from jax.experimental.pallas import tpu as pltpu

F32 = jnp.float32
BF16 = jnp.bfloat16

D_MODEL = 4096
DEPTH = 4
N_A_LAYERS = DEPTH // 2
RET_HEADS = 16
RET_QK_DIM = D_MODEL // RET_HEADS
RET_V_DIM = 2 * D_MODEL // RET_HEADS
RET_CHUNK = 128
DIFF_HEADS = 16
DIFF_HEAD_DIM = D_MODEL // (2 * DIFF_HEADS)
DIFF_V_DIM = 2 * DIFF_HEAD_DIM
ROPE_THETA = 10000.0
N_EXPERTS = 8
D_EXPERT = D_MODEL // 4
PLE_DIM = 256
LN_EPS = 1e-5
DEEPNORM_ALPHA = (2.0 * DEPTH) ** 0.25

V7X_LANES = 128
V7X_VMEM_BYTES = 64 * 1024 * 1024
V7X_VMEM_CAP = V7X_VMEM_BYTES - 6 * 1024 * 1024
V7X_VMEM_FLOOR = 16 * 1024 * 1024
V7X_BF16_SUBLANES = 16
VMEM_SLACK = 2 * 1024 * 1024

MM_BM = 1024
MM_BN_WIDE_K = 256
MM_BN = 512
MM_BN_WIDE = 1024
LN_ROWS = 512
RET_ROWS = 2048
ATT_TILE = 1024
ATT_STRIP = 64
ATT_SCORE_SLOTS = 3
LOG2_E = math.log2(math.e)


def _nbytes(shape, dtype):
    return math.prod(shape) * jnp.dtype(dtype).itemsize


def _compiler_params(semantics, block_bytes, temp_bytes):
    need = 2 * block_bytes + temp_bytes + VMEM_SLACK
    return pltpu.CompilerParams(
        dimension_semantics=semantics,
        vmem_limit_bytes=int(min(max(need, V7X_VMEM_FLOOR), V7X_VMEM_CAP)),
    )


def _w_spec(w, layer, k, bn, col_block_off=0):
    if w.ndim == 3:
        return pl.BlockSpec((None, k, bn), lambda i, j: (layer, 0, j + col_block_off))
    return pl.BlockSpec((k, bn), lambda i, j: (0, j + col_block_off))


def _hosting_cast(body, n_in):
    def hosted(*refs):
        body(*refs[:n_in], refs[n_in + 1])
        refs[n_in + 2][...] = refs[n_in][...].astype(BF16)
    return hosted


def _launch_matmul(body, name, grid, in_specs, args, out_block, out_shape, block_bytes, temp_bytes, cast=None):
    out_spec = pl.BlockSpec(out_block, lambda i, j: (i, j))
    semantics = ("parallel", "arbitrary")
    if cast is None:
        return pl.pallas_call(
            body, grid=grid, in_specs=in_specs, out_specs=out_spec, out_shape=out_shape,
            compiler_params=_compiler_params(semantics, block_bytes, temp_bytes), name=name,
        )(*args)
    src, part, n_parts = cast
    rows, cols = src.shape[0] // n_parts, src.shape[1]
    steps = grid[0] * grid[1]
    slab = rows // steps
    assert src.shape[0] % n_parts == 0 and rows % steps == 0 and slab % V7X_BF16_SUBLANES == 0
    block_bytes += _nbytes((slab, cols), F32) + _nbytes((slab, cols), BF16)
    return pl.pallas_call(
        _hosting_cast(body, len(in_specs)),
        grid=grid,
        in_specs=list(in_specs) + [pl.BlockSpec((slab, cols), lambda i, j: (part * steps + i * grid[1] + j, 0))],
        out_specs=[out_spec, pl.BlockSpec((slab, cols), lambda i, j: (i * grid[1] + j, 0))],
        out_shape=[out_shape, jax.ShapeDtypeStruct((rows, cols), BF16)],
        compiler_params=_compiler_params(semantics, block_bytes, temp_bytes), name=name + "_cast",
    )(*args, src)


def _mm_plain_body(a_ref, w_ref, o_ref):
    o_ref[...] = jnp.dot(a_ref[...], w_ref[...], preferred_element_type=F32).astype(o_ref.dtype)


def matmul(a, w, *, layer=None, col_off=0, n_out, out_dtype, cast=None):
    m, k = a.shape
    bm, bn = MM_BM, (MM_BN_WIDE if k <= D_MODEL else MM_BN_WIDE_K)
    bm = min(bm, m)
    assert m % bm == 0 and n_out % bn == 0 and col_off % bn == 0
    blocks = _nbytes((bm, k), a.dtype) + _nbytes((k, bn), w.dtype) + _nbytes((bm, bn), out_dtype)
    return _launch_matmul(
        _mm_plain_body, "matmul", (m // bm, n_out // bn),
        [pl.BlockSpec((bm, k), lambda i, j: (i, 0)), _w_spec(w, layer, k, bn, col_off // bn)], (a, w),
        (bm, bn), jax.ShapeDtypeStruct((m, n_out), out_dtype), blocks, 2 * _nbytes((bm, bn), F32), cast)


class Normed(NamedTuple):
    y: jax.Array
    mean: jax.Array
    rstd: jax.Array
    gain: jax.Array
    bias: jax.Array
    row: int


def _residual_operands(res, bm, bn):
    tile = pl.BlockSpec((bm, bn), lambda i, j: (i, j))
    if not isinstance(res, Normed):
        return [tile], (res,), _nbytes((bm, bn), F32)
    stat = pl.BlockSpec((bm, V7X_LANES), lambda i, j: (i, 0))
    vec = pl.BlockSpec((None, 1, bn), lambda i, j: (res.row, 0, j))
    nbytes = _nbytes((bm, bn), F32) + 2 * _nbytes((bm, V7X_LANES), F32) + 2 * _nbytes((8, bn), F32)
    return [tile, stat, stat, vec, vec], (res.y, res.mean, res.rstd, res.gain, res.bias), nbytes


def _residual_tile(refs):
    if len(refs) == 1:
        return refs[0][...]
    y_ref, mean_ref, rstd_ref, gain_ref, bias_ref = refs
    reps = y_ref.shape[1] // V7X_LANES
    mean = jnp.concatenate([mean_ref[...]] * reps, axis=1)
    rstd = jnp.concatenate([rstd_ref[...]] * reps, axis=1)
    return (y_ref[...] - mean) * rstd * gain_ref[...] + bias_ref[...]


def _mm_residual_body(a_ref, w_ref, *refs):
    *res_refs, o_ref = refs
    o_ref[...] = (DEEPNORM_ALPHA * _residual_tile(res_refs)
                  + jnp.dot(a_ref[...], w_ref[...], preferred_element_type=F32))


def matmul_residual(a, w, res, *, layer, cast=None):
    m, k = a.shape
    n = w.shape[-1]
    bm, bn = MM_BM, (MM_BN if k <= D_MODEL else MM_BN_WIDE_K)
    bm = min(bm, m)
    assert m % bm == 0 and n % bn == 0
    res_specs, res_args, res_bytes = _residual_operands(res, bm, bn)
    blocks = _nbytes((bm, k), a.dtype) + _nbytes((k, bn), w.dtype) + _nbytes((bm, bn), F32) + res_bytes
    return _launch_matmul(
        _mm_residual_body, "matmul_residual", (m // bm, n // bn),
        [pl.BlockSpec((bm, k), lambda i, j: (i, 0)), _w_spec(w, layer, k, bn)] + res_specs, (a, w) + res_args,
        (bm, bn), jax.ShapeDtypeStruct((m, n), F32), blocks, 3 * _nbytes((bm, bn), F32), cast)


def _mm_rope_body(a_ref, w_ref, cos_ref, sin_ref, o_ref, *, head_dim, split_block, scale_lo, scale_hi):
    acc = jnp.dot(a_ref[...], w_ref[...], preferred_element_type=F32)
    cos = cos_ref[...]
    sin = sin_ref[...]
    if scale_lo == scale_hi:
        scale = scale_lo
    else:
        scale = jnp.where(pl.program_id(1) < split_block, scale_lo, scale_hi)
    for h in range(acc.shape[1] // head_dim):
        cols = slice(h * head_dim, (h + 1) * head_dim)
        xb = acc[:, cols]
        rot = pltpu.roll(xb, head_dim // 2, axis=1)
        o_ref[:, cols] = ((xb * cos + rot * sin) * scale).astype(o_ref.dtype)


def matmul_rope(a, w, cos_full, sin_signed, *, layer=None, col_off=0, n_out, head_dim,
                split_col=0, scale_lo=1.0, scale_hi=1.0, bm=MM_BM, bn=MM_BN_WIDE):
    m, k = a.shape
    seq = cos_full.shape[0]
    bm = min(bm, seq)
    assert m % bm == 0 and seq % bm == 0 and n_out % bn == 0 and bn % head_dim == 0
    assert col_off % bn == 0 and split_col % bn == 0
    pos_tiles = seq // bm
    blocks = (_nbytes((bm, k), a.dtype) + _nbytes((k, bn), w.dtype) + _nbytes((bm, bn), BF16)
              + 2 * _nbytes((bm, head_dim), F32))
    body = functools.partial(_mm_rope_body, head_dim=head_dim, split_block=split_col // bn,
                             scale_lo=scale_lo, scale_hi=scale_hi)
    return _launch_matmul(
        body, "matmul_rope", (m // bm, n_out // bn),
        [pl.BlockSpec((bm, k), lambda i, j: (i, 0)),
         _w_spec(w, layer, k, bn, col_off // bn),
         pl.BlockSpec((bm, head_dim), lambda i, j: (i % pos_tiles, 0)),
         pl.BlockSpec((bm, head_dim), lambda i, j: (i % pos_tiles, 0))], (a, w, cos_full, sin_signed),
        (bm, bn), jax.ShapeDtypeStruct((m, n_out), BF16), blocks, 3 * _nbytes((bm, bn), F32))


def _mm_swiglu_body(a_ref, wa_ref, wb_ref, o_ref):
    a = a_ref[...]
    ga = jnp.dot(a, wa_ref[...], preferred_element_type=F32)
    gb = jnp.dot(a, wb_ref[...], preferred_element_type=F32)
    o_ref[...] = (jax.nn.silu(ga) * gb).astype(o_ref.dtype)


def matmul_swiglu(a, w13, *, layer=None, bm=MM_BM, bn=MM_BN, cast=None):
    m, k = a.shape
    f = w13.shape[-1] // 2
    bm = min(bm, m)
    assert m % bm == 0 and f % bn == 0
    blocks = _nbytes((bm, k), a.dtype) + 2 * _nbytes((k, bn), w13.dtype) + _nbytes((bm, bn), BF16)
    return _launch_matmul(
        _mm_swiglu_body, "matmul_swiglu", (m // bm, f // bn),
        [pl.BlockSpec((bm, k), lambda i, j: (i, 0)), _w_spec(w13, layer, k, bn, 0),
         _w_spec(w13, layer, k, bn, f // bn)], (a, w13, w13),
        (bm, bn), jax.ShapeDtypeStruct((m, f), BF16), blocks, 4 * _nbytes((bm, bn), F32), cast)


def _mm_moe_swiglu_body(a_ref, wa_ref, wb_ref, gates_ref, o_ref, *, blocks_per_expert):
    a = a_ref[...]
    ga = jnp.dot(a, wa_ref[...], preferred_element_type=F32)
    gb = jnp.dot(a, wb_ref[...], preferred_element_type=F32)
    expert = pl.program_id(1) // blocks_per_expert
    gates = gates_ref[...]
    lane = lax.broadcasted_iota(jnp.int32, gates.shape, 1)
    gate = jnp.sum(jnp.where(lane == expert, gates, 0.0), axis=1, keepdims=True)
    o_ref[...] = (gate * (jax.nn.silu(ga) * gb)).astype(o_ref.dtype)


def matmul_moe_swiglu(a, w13, gates, *, bm=MM_BM, bn=MM_BN, cast=None):
    m, k = a.shape
    n_exp, f = w13.shape[0], w13.shape[-1] // 2
    bm = min(bm, m)
    assert m % bm == 0 and f % bn == 0
    bpe = f // bn
    blocks = (_nbytes((bm, k), a.dtype) + 2 * _nbytes((k, bn), w13.dtype) + _nbytes((bm, bn), BF16)
              + _nbytes((bm, V7X_LANES), F32))
    return _launch_matmul(
        functools.partial(_mm_moe_swiglu_body, blocks_per_expert=bpe), "matmul_moe_swiglu",
        (m // bm, n_exp * bpe),
        [pl.BlockSpec((bm, k), lambda i, j: (i, 0)),
         pl.BlockSpec((None, k, bn), lambda i, j: (j // bpe, 0, j % bpe)),
         pl.BlockSpec((None, k, bn), lambda i, j: (j // bpe, 0, j % bpe + bpe)),
         pl.BlockSpec((bm, V7X_LANES), lambda i, j: (i, 0))], (a, w13, w13, gates),
        (bm, bn), jax.ShapeDtypeStruct((m, n_exp * f), BF16), blocks, 4 * _nbytes((bm, bn), F32), cast)


def _mm_ple_body(a_ref, wg_ref, p_ref, wp_ref, *refs):
    *res_refs, o_ref = refs
    gate = jnp.dot(a_ref[...], wg_ref[...], preferred_element_type=F32)
    proj = jnp.dot(p_ref[...].astype(BF16), wp_ref[...], preferred_element_type=F32)
    o_ref[...] = DEEPNORM_ALPHA * _residual_tile(res_refs) + jax.nn.sigmoid(gate) * proj


def matmul_ple_residual(a, w_gate, p, w_proj, res, *, layer, bm=MM_BM, bn=MM_BN, cast=None):
    m, k = a.shape
    n = w_gate.shape[-1]
    kp = p.shape[-1]
    bm = min(bm, m)
    assert m % bm == 0 and n % bn == 0
    res_specs, res_args, res_bytes = _residual_operands(res, bm, bn)
    blocks = (_nbytes((bm, k), a.dtype) + _nbytes((k, bn), w_gate.dtype) + _nbytes((bm, kp), p.dtype)
              + _nbytes((kp, bn), w_proj.dtype) + _nbytes((bm, bn), F32) + res_bytes)
    return _launch_matmul(
        _mm_ple_body, "matmul_ple_residual", (m // bm, n // bn),
        [pl.BlockSpec((bm, k), lambda i, j: (i, 0)),
         _w_spec(w_gate, layer, k, bn),
         pl.BlockSpec((None, bm, kp), lambda i, j: (layer, i, 0)),
         _w_spec(w_proj, layer, kp, bn)] + res_specs, (a, w_gate, p, w_proj) + res_args,
        (bm, bn), jax.ShapeDtypeStruct((m, n), F32), blocks, 4 * _nbytes((bm, bn), F32), cast)


def _ln_normalise(y_ref, g_ref, b_ref):
    y = y_ref[...]
    mean = jnp.mean(y, axis=-1, keepdims=True)
    d = y - mean
    rstd = lax.rsqrt(jnp.mean(d * d, axis=-1, keepdims=True) + LN_EPS)
    return d * rstd * g_ref[...] + b_ref[...], mean, rstd


def _ln_stats_body(y_ref, g_ref, b_ref, ob_ref, mean_ref, rstd_ref):
    out, mean, rstd = _ln_normalise(y_ref, g_ref, b_ref)
    ob_ref[...] = out.astype(BF16)
    mean_ref[...] = jnp.broadcast_to(mean, mean_ref.shape)
    rstd_ref[...] = jnp.broadcast_to(rstd, rstd_ref.shape)


def _ln_f32_body(y_ref, g_ref, b_ref, o_ref):
    o_ref[...] = _ln_normalise(y_ref, g_ref, b_ref)[0]


def layer_norm(y, ln_g, ln_b, *, row, final=False, rows=LN_ROWS):
    m, d = y.shape
    rows = min(rows, m)
    assert m % rows == 0
    tile = pl.BlockSpec((rows, d), lambda i: (i, 0))
    stat = pl.BlockSpec((rows, V7X_LANES), lambda i: (i, 0))
    vec = pl.BlockSpec((None, 1, d), lambda i: (row, 0, 0))
    blocks = _nbytes((rows, d), F32) + 2 * _nbytes((8, d), F32)
    if final:
        return pl.pallas_call(
            _ln_f32_body, grid=(m // rows,), in_specs=[tile, vec, vec], out_specs=tile,
            out_shape=jax.ShapeDtypeStruct((m, d), F32),
            compiler_params=_compiler_params(("parallel",), blocks + _nbytes((rows, d), F32),
                                             3 * _nbytes((rows, d), F32)),
            name="layer_norm_f32",
        )(y, ln_g, ln_b)
    xb, mean, rstd = pl.pallas_call(
        _ln_stats_body, grid=(m // rows,), in_specs=[tile, vec, vec], out_specs=[tile, stat, stat],
        out_shape=[jax.ShapeDtypeStruct((m, d), BF16), jax.ShapeDtypeStruct((m, V7X_LANES), F32),
                   jax.ShapeDtypeStruct((m, V7X_LANES), F32)],
        compiler_params=_compiler_params(("parallel",), blocks + _nbytes((rows, d), BF16)
                                         + 2 * _nbytes((rows, V7X_LANES), F32), 3 * _nbytes((rows, d), F32)),
        name="layer_norm",
    )(y, ln_g, ln_b)
    return Normed(y, mean, rstd, ln_g, ln_b, row), xb


def _retention_body(q_ref, k_ref, v_ref, g_ref, dmask_ref, qdec_ref, kdec_ref, cdec_ref, o_ref, state_ref,
                    *, n_chunks, chunk):
    @pl.when(pl.program_id(2) == 0)
    def _():
        state_ref[...] = jnp.zeros_like(state_ref)

    dmask = dmask_ref[...]
    qdec = qdec_ref[...]
    kdec = kdec_ref[...]
    cdec = cdec_ref[...]
    for c in range(n_chunks):
        rows = pl.ds(c * chunk, chunk)
        q = q_ref[rows, :]
        k = k_ref[rows, :]
        v = v_ref[rows, :]
        state = state_ref[...]
        scores = lax.dot_general(q, k, (((1,), (1,)), ((), ())), preferred_element_type=F32) * dmask
        inner = jnp.dot(scores.astype(BF16), v, preferred_element_type=F32)
        cross = jnp.dot((q.astype(F32) * qdec).astype(BF16), state.astype(BF16), preferred_element_type=F32)
        k_scaled = (k.astype(F32) * kdec).astype(BF16)
        update = lax.dot_general(k_scaled, v, (((0,), (0,)), ((), ())), preferred_element_type=F32)
        state_ref[...] = state * cdec + update
        out = inner + cross
        mu = jnp.mean(out, axis=-1, keepdims=True)
        d = out - mu
        var = jnp.mean(d * d, axis=-1, keepdims=True)
        normed = d * lax.rsqrt(var + LN_EPS)
        gate = g_ref[rows, :].astype(F32)
        o_ref[rows, :] = (jax.nn.silu(gate) * normed).astype(o_ref.dtype)


def retention_tables():
    h, c = RET_HEADS, RET_CHUNK
    log_gamma = jnp.log1p(-jnp.exp2(-5.0 - jnp.arange(h, dtype=F32)))
    pos = jnp.arange(c, dtype=F32)
    rel = pos[:, None] - pos[None, :]
    dmask = jnp.where(rel >= 0, jnp.exp(log_gamma[:, None, None] * jnp.maximum(rel, 0.0)), 0.0)
    qdec = jnp.exp(log_gamma[:, None] * (pos + 1.0))[:, :, None]
    kdec = jnp.exp(log_gamma[:, None] * (c - 1.0 - pos))[:, :, None]
    cdec = jnp.exp(log_gamma * c)[:, None, None]
    return (dmask,
            jnp.broadcast_to(qdec, (h, c, RET_QK_DIM)),
            jnp.broadcast_to(kdec, (h, c, RET_QK_DIM)),
            jnp.broadcast_to(cdec, (h, 1, RET_V_DIM)))


def retention(qk, vg, tables, *, batch, seq, rows=RET_ROWS):
    h, dk, dv, c = RET_HEADS, RET_QK_DIM, RET_V_DIM, RET_CHUNK
    rows = min(rows, seq)
    assert seq % rows == 0 and rows % c == 0
    steps = seq // rows
    dmask, qdec, kdec, cdec = tables
    row_tile = lambda b, hh, l: b * steps + l
    blocks = (2 * _nbytes((rows, dk), BF16) + 3 * _nbytes((rows, dv), BF16) + _nbytes((c, c), F32)
              + 2 * _nbytes((c, dk), F32) + _nbytes((8, dv), F32))
    return pl.pallas_call(
        functools.partial(_retention_body, n_chunks=rows // c, chunk=c),
        grid=(batch, h, steps),
        in_specs=[pl.BlockSpec((rows, dk), lambda b, hh, l: (row_tile(b, hh, l), hh)),
                  pl.BlockSpec((rows, dk), lambda b, hh, l: (row_tile(b, hh, l), h + hh)),
                  pl.BlockSpec((rows, dv), lambda b, hh, l: (row_tile(b, hh, l), hh)),
                  pl.BlockSpec((rows, dv), lambda b, hh, l: (row_tile(b, hh, l), h + hh)),
                  pl.BlockSpec((None, c, c), lambda b, hh, l: (hh, 0, 0)),
                  pl.BlockSpec((None, c, dk), lambda b, hh, l: (hh, 0, 0)),
                  pl.BlockSpec((None, c, dk), lambda b, hh, l: (hh, 0, 0)),
                  pl.BlockSpec((None, 1, dv), lambda b, hh, l: (hh, 0, 0))],
        out_specs=pl.BlockSpec((rows, dv), lambda b, hh, l: (row_tile(b, hh, l), hh)),
        out_shape=jax.ShapeDtypeStruct((batch * seq, h * dv), BF16),
        scratch_shapes=[pltpu.VMEM((dk, dv), F32)],
        compiler_params=_compiler_params(("parallel", "parallel", "arbitrary"), blocks,
                                         _nbytes((dk, dv), F32) * 4 + _nbytes((rows, dv), F32) * 4),
        name="retention",
    )(qk, qk, vg, vg, dmask, qdec, kdec, cdec)


def _diff_attention_body(q_ref, k_ref, v_ref, lam_ref, g_ref, o_ref, max_ref, sum_ref, acc_ref, s_ref, p_ref,
                         *, tile, strip, lambda_init):
    qi = pl.program_id(2)
    d = DIFF_HEAD_DIM
    lanes = V7X_LANES
    contract_last = (((1,), (1,)), ((), ()))

    half = tile // 2

    def row_groups(diagonal):
        if diagonal:
            return [(slice(0, half), half), (slice(half, tile), tile)]
        return [(slice(0, half), tile), (slice(half, tile), tile)]

    n_cache = s_ref.shape[0] - 1

    def scores(kt, diagonal, slot):
        base = pl.multiple_of(kt * tile, tile)
        for c in range(2):
            for rows, width in row_groups(diagonal):
                s_ref[slot, c, rows, :width] = lax.dot_general(
                    q_ref[rows, c * d:(c + 1) * d], k_ref[pl.ds(base, width), c * d:(c + 1) * d],
                    contract_last, preferred_element_type=F32)

    def block(slot, c, r, cb, diagonal):
        if diagonal and cb * lanes >= (r + 1) * strip:
            return None
        s = s_ref[slot, c, r * strip:(r + 1) * strip, cb * lanes:(cb + 1) * lanes]
        if diagonal and (cb + 1) * lanes - 1 > r * strip:
            row_id = r * strip + lax.broadcasted_iota(jnp.int32, (strip, lanes), 0)
            col_id = cb * lanes + lax.broadcasted_iota(jnp.int32, (strip, lanes), 1)
            s = jnp.where(row_id >= col_id, s, -jnp.inf)
        return s

    def max_sweep(kt, diagonal, slot):
        scores(kt, diagonal, slot)
        for c in range(2):
            for r in range(tile // strip):
                rows = slice(r * strip, (r + 1) * strip)
                blocks = [block(slot, c, r, cb, diagonal) for cb in range(tile // lanes)]
                max_ref[c, rows] = functools.reduce(jnp.maximum,
                                                    [max_ref[c, rows]] + [b for b in blocks if b is not None])

    def sum_sweep(kt, diagonal, slot, recompute):
        if recompute:
            scores(kt, diagonal, slot)
        base = pl.multiple_of(kt * tile, tile)
        for c in range(2):
            for r in range(tile // strip):
                rows = slice(r * strip, (r + 1) * strip)
                row_max = max_ref[c, rows]
                total = sum_ref[c, rows]
                read_cols = half if diagonal and (r + 1) * strip <= half else tile
                for cb in range(read_cols // lanes):
                    s = block(slot, c, r, cb, diagonal)
                    if s is None:
                        p_ref[c, rows, cb * lanes:(cb + 1) * lanes] = jnp.zeros((strip, lanes), BF16)
                        continue
                    p = jnp.exp2(s - row_max)
                    total = total + p
                    p_ref[c, rows, cb * lanes:(cb + 1) * lanes] = p.astype(BF16)
                sum_ref[c, rows] = total
        for c in range(2):
            for rows, width in row_groups(diagonal):
                acc_ref[c, rows] += jnp.dot(p_ref[c, rows, :width], v_ref[pl.ds(base, width), :],
                                            preferred_element_type=F32)

    def for_key_tiles(lo, hi, step):
        def body(kt, carry):
            step(kt)
            return carry
        lax.fori_loop(lo, hi, body, 0)

    max_ref[...] = jnp.full_like(max_ref, -jnp.inf)
    sum_ref[...] = jnp.zeros_like(sum_ref)
    acc_ref[...] = jnp.zeros_like(acc_ref)
    for_key_tiles(0, qi, lambda kt: max_sweep(kt, False, jnp.minimum(kt, n_cache)))
    max_sweep(qi, True, n_cache)
    for c in range(2):
        max_ref[c] = jnp.broadcast_to(jnp.max(max_ref[c], axis=1, keepdims=True), (tile, lanes))
    sum_sweep(qi, True, n_cache, False)
    for_key_tiles(0, jnp.minimum(qi, n_cache), lambda kt: sum_sweep(kt, False, kt, False))
    for_key_tiles(n_cache, qi, lambda kt: sum_sweep(kt, False, n_cache, True))

    lam = lam_ref[...]
    lam_full = (jnp.exp(jnp.sum(lam[0:1] * lam[1:2], axis=1, keepdims=True))
                - jnp.exp(jnp.sum(lam[2:3] * lam[3:4], axis=1, keepdims=True)) + lambda_init)
    inv_l = [1.0 / jnp.sum(sum_ref[c], axis=1, keepdims=True) for c in range(2)]
    out = acc_ref[0] * inv_l[0] - lam_full * (acc_ref[1] * inv_l[1])
    y = out * lax.rsqrt(jnp.mean(out * out, axis=-1, keepdims=True) + LN_EPS)
    o_ref[...] = ((y * g_ref[...]) * (1.0 - lambda_init)).astype(o_ref.dtype)


def diff_attention(q, k, v, lam, subln_g, *, layer, batch, seq, lambda_init, tile=ATT_TILE, strip=ATT_STRIP):
    h, dv = DIFF_HEADS, DIFF_V_DIM
    tile = min(tile, seq)
    assert seq % tile == 0 and (tile // 2) % strip == 0 and (tile // 2) % V7X_LANES == 0
    nq = seq // tile
    blocks = (2 * _nbytes((tile, dv), BF16) + 2 * _nbytes((seq, dv), BF16) + _nbytes((8, V7X_LANES), F32)
              + _nbytes((8, dv), F32))
    scratch = [pltpu.VMEM((2, tile, V7X_LANES), F32), pltpu.VMEM((2, tile, V7X_LANES), F32),
               pltpu.VMEM((2, tile, dv), F32), pltpu.VMEM((ATT_SCORE_SLOTS, 2, tile, tile), F32),
               pltpu.VMEM((2, tile, tile), BF16)]
    scratch_bytes = (2 * 2 * _nbytes((tile, V7X_LANES), F32) + 2 * _nbytes((tile, dv), F32)
                     + ATT_SCORE_SLOTS * 2 * _nbytes((tile, tile), F32) + 2 * _nbytes((tile, tile), BF16))
    return pl.pallas_call(
        functools.partial(_diff_attention_body, tile=tile, strip=strip, lambda_init=lambda_init),
        grid=(batch, h, nq),
        in_specs=[pl.BlockSpec((tile, dv), lambda b, hh, qi: (b * nq + qi, hh)),
                  pl.BlockSpec((seq, dv), lambda b, hh, qi: (b, hh)),
                  pl.BlockSpec((seq, dv), lambda b, hh, qi: (b, hh)),
                  pl.BlockSpec((None, 4, DIFF_HEAD_DIM), lambda b, hh, qi: (layer, 0, 0)),
                  pl.BlockSpec((None, 1, dv), lambda b, hh, qi: (layer, 0, 0))],
        out_specs=pl.BlockSpec((tile, dv), lambda b, hh, qi: (b * nq + qi, hh)),
        out_shape=jax.ShapeDtypeStruct((batch * seq, h * dv), BF16),
        scratch_shapes=scratch,
        compiler_params=_compiler_params(("parallel", "parallel", "arbitrary"), blocks,
                                         scratch_bytes + 3 * _nbytes((tile, tile), F32)),
        name="diff_attention",
    )(q, k, v, lam, subln_g)


def _router_body(a_ref, w_ref, o_ref):
    logits = jnp.dot(a_ref[...], w_ref[...], preferred_element_type=F32)
    lane = lax.broadcasted_iota(jnp.int32, logits.shape, 1).astype(F32)
    logits = jnp.where(lane < N_EXPERTS, logits, -jnp.inf)
    top1 = jnp.max(logits, axis=1, keepdims=True)
    idx1 = jnp.min(jnp.where(logits == top1, lane, float(V7X_LANES)), axis=1, keepdims=True)
    rest = jnp.where(lane == idx1, -jnp.inf, logits)
    top2 = jnp.max(rest, axis=1, keepdims=True)
    idx2 = jnp.min(jnp.where(rest == top2, lane, float(V7X_LANES)), axis=1, keepdims=True)
    e = jnp.exp(top2 - top1)
    w1 = 1.0 / (1.0 + e)
    w2 = e / (1.0 + e)
    o_ref[...] = jnp.where(lane == idx1, w1, 0.0) + jnp.where(lane == idx2, w2, 0.0)


def router_gates(a, w_router_padded, *, layer, bm=MM_BM):
    m, k = a.shape
    bm = min(bm, m)
    assert m % bm == 0
    blocks = _nbytes((bm, k), a.dtype) + _nbytes((k, V7X_LANES), BF16) + _nbytes((bm, V7X_LANES), F32)
    return pl.pallas_call(
        _router_body,
        grid=(m // bm,),
        in_specs=[pl.BlockSpec((bm, k), lambda i: (i, 0)),
                  pl.BlockSpec((None, k, V7X_LANES), lambda i: (layer, 0, 0))],
        out_specs=pl.BlockSpec((bm, V7X_LANES), lambda i: (i, 0)),
        out_shape=jax.ShapeDtypeStruct((m, V7X_LANES), F32),
        compiler_params=_compiler_params(("parallel",), blocks, 8 * _nbytes((bm, V7X_LANES), F32)),
        name="router_gates",
    )(a, w_router_padded)


def _rope_tables(seq, dim):
    inv_freq = 1.0 / (ROPE_THETA ** (jnp.arange(0, dim, 2, dtype=F32) / dim))
    ang = jnp.arange(seq, dtype=F32)[:, None] * inv_freq[None, :]
    cos, sin = jnp.cos(ang), jnp.sin(ang)
    return jnp.concatenate([cos, cos], axis=-1), jnp.concatenate([-sin, sin], axis=-1)


def kernel(x, p, ret_w_in, ret_w_out, kv_w, diff_w_q, diff_w_out, diff_lambda, diff_subln_g, ffn_w13, ffn_w2,
           moe_router, moe_w13, moe_w2, ple_w_gate, ple_w_proj, ln_g, ln_b):
    batch, seq, d = x.shape
    assert d == D_MODEL
    tokens = batch * seq
    cos_r, sin_r = _rope_tables(seq, RET_QK_DIM)
    cos_d, sin_d = _rope_tables(seq, DIFF_HEAD_DIM)
    ret_tabs = retention_tables()

    w_in0 = ret_w_in[0].astype(BF16)
    ple_w_proj = ple_w_proj.astype(BF16)
    router_w = jnp.pad(moe_router.astype(BF16), ((0, 0), (0, 0), (0, V7X_LANES - N_EXPERTS)))
    flat = lambda w: w.reshape(-1, w.shape[-1])
    n_ret, n_dense, n_moe = ret_w_out.shape[0], ffn_w13.shape[0], moe_w13.shape[0]
    ln_g = ln_g.reshape(DEPTH * 3, 1, D_MODEL)
    ln_b = ln_b.reshape(DEPTH * 3, 1, D_MODEL)
    subln_g = diff_subln_g.reshape(-1, 1, DIFF_V_DIM)
    p = p.reshape(DEPTH, tokens, PLE_DIM)

    x = x.reshape(tokens, D_MODEL)
    xb = x.astype(BF16)
    qk_w = RET_HEADS * RET_QK_DIM
    v_w = RET_HEADS * RET_V_DIM
    kd = 2 * DIFF_HEADS * DIFF_HEAD_DIM
    vd = DIFF_HEADS * DIFF_V_DIM
    q_scale = DIFF_HEAD_DIM ** -0.5 * LOG2_E

    def retention_layer(i, w_in, w_out, x, xb, vg_cast, out_cast):
        qk = matmul_rope(xb, w_in, cos_r, sin_r, col_off=0, n_out=2 * qk_w, head_dim=RET_QK_DIM,
                         split_col=qk_w, scale_lo=1.0, scale_hi=RET_QK_DIM ** -0.5)
        vg, cast_a = matmul(xb, w_in, col_off=2 * qk_w, n_out=2 * v_w, out_dtype=BF16, cast=vg_cast)
        w_out = w_out if w_out is not None else cast_a.reshape(ret_w_out.shape)
        gated = retention(qk, vg, ret_tabs, batch=batch, seq=seq)
        y, cast_b = matmul_residual(gated, w_out, x, layer=i, cast=out_cast)
        return y, w_out, cast_a, cast_b

    def attention_layer(j, w_q, w_out, k_sh, v_sh, x, xb, out_cast):
        lambda_init = 0.8 - 0.6 * math.exp(-0.3 * (j + N_A_LAYERS))
        q = matmul_rope(xb, w_q, cos_d, sin_d, layer=j, n_out=kd, head_dim=DIFF_HEAD_DIM,
                        scale_lo=q_scale, scale_hi=q_scale)
        att = diff_attention(q, k_sh, v_sh, diff_lambda, subln_g, layer=j, batch=batch, seq=seq,
                             lambda_init=lambda_init)
        return matmul_residual(att, w_out, x, layer=j, cast=out_cast)

    y, w_ret_out, _, w13_d0 = retention_layer(0, w_in0, None, x, xb, (flat(ret_w_out), 0, 1),
                                               (flat(ffn_w13), 0, n_dense))
    x, xb = layer_norm(y, ln_g, ln_b, row=0)
    hidden, w2_d = matmul_swiglu(xb, w13_d0, cast=(flat(ffn_w2), 0, 1))
    w2_d = w2_d.reshape(ffn_w2.shape)
    y, w_ple = matmul_residual(hidden, w2_d, x, layer=0, cast=(flat(ple_w_gate), 0, 1))
    w_ple = w_ple.reshape(ple_w_gate.shape)
    x, xb = layer_norm(y, ln_g, ln_b, row=1)
    y, w_in1 = matmul_ple_residual(xb, w_ple, p, ple_w_proj, x, layer=0, cast=(flat(ret_w_in), 1, n_ret))
    x, xb = layer_norm(y, ln_g, ln_b, row=2)

    y, _, w13_m0, w2_m = retention_layer(1, w_in1, w_ret_out, x, xb, (flat(moe_w13), 0, n_moe),
                                         (flat(moe_w2), 0, 1))
    w13_m0 = w13_m0.reshape(moe_w13.shape[1:])
    w2_m = w2_m.reshape(n_moe, N_EXPERTS * D_EXPERT, D_MODEL)
    x, xb = layer_norm(y, ln_g, ln_b, row=3)
    gates = router_gates(xb, router_w, layer=0)
    hidden, w_kv = matmul_moe_swiglu(xb, w13_m0, gates, cast=(kv_w, 0, 1))
    y, w_q = matmul_residual(hidden, w2_m, x, layer=0, cast=(flat(diff_w_q), 0, 1))
    w_q = w_q.reshape(diff_w_q.shape)
    x, xb = layer_norm(y, ln_g, ln_b, row=4)
    y, w_att_out = matmul_ple_residual(xb, w_ple, p, ple_w_proj, x, layer=1, cast=(flat(diff_w_out), 0, 1))
    w_att_out = w_att_out.reshape(diff_w_out.shape)
    x, xb = layer_norm(y, ln_g, ln_b, row=5)
    k_sh = matmul_rope(xb, w_kv, cos_d, sin_d, col_off=0, n_out=kd, head_dim=DIFF_HEAD_DIM)
    v_sh = matmul(xb, w_kv, col_off=kd, n_out=vd, out_dtype=BF16)

    y, w13_d1 = attention_layer(0, w_q, w_att_out, k_sh, v_sh, x, xb, (flat(ffn_w13), 1, n_dense))
    x, xb = layer_norm(y, ln_g, ln_b, row=6)
    hidden, w13_m1 = matmul_swiglu(xb, w13_d1, cast=(flat(moe_w13), 1, n_moe))
    w13_m1 = w13_m1.reshape(moe_w13.shape[1:])
    y = matmul_residual(hidden, w2_d, x, layer=1)
    x, xb = layer_norm(y, ln_g, ln_b, row=7)
    y = matmul_ple_residual(xb, w_ple, p, ple_w_proj, x, layer=2)
    x, xb = layer_norm(y, ln_g, ln_b, row=8)

    y = attention_layer(1, w_q, w_att_out, k_sh, v_sh, x, xb, None)
    x, xb = layer_norm(y, ln_g, ln_b, row=9)
    gates = router_gates(xb, router_w, layer=1)
    hidden = matmul_moe_swiglu(xb, w13_m1, gates)
    y = matmul_residual(hidden, w2_m, x, layer=1)
    x, xb = layer_norm(y, ln_g, ln_b, row=10)
    y = matmul_ple_residual(xb, w_ple, p, ple_w_proj, x, layer=3)
    out = layer_norm(y, ln_g, ln_b, row=11, final=True)
    return out.reshape(batch, seq, D_MODEL)
```
